```python
import math
import jax
import jax.numpy as jnp
from jax import lax
import numpy as np

D_MODEL = 2048
BATCH = 16
SEQ = 2048
DEPTH = 4

NORM_EPS = 1e-6
LN_EPS = 1e-5
A_HEADS = 8
A_HEAD_DIM = 128
A_WIDTH = A_HEADS * A_HEAD_DIM
KV_RANK = 512
IDX_HEADS = 16
IDX_DIM = 64
TOPK_MAX = 256
Q_BLOCK = 128
B_WIDTH = 1024
CONV_K = 31
C_HEADS = 16
C_HEAD_DIM = 128
C_WIDTH = C_HEADS * C_HEAD_DIM
SHORT_K = 4
CHUNK = 64
N_BRANCH = 3
SPLITS = (
    A_WIDTH, KV_RANK, IDX_HEADS * IDX_DIM, IDX_DIM, IDX_HEADS, A_WIDTH,
    B_WIDTH, B_WIDTH, B_WIDTH,
    3 * C_WIDTH, C_HEADS, C_HEADS, C_WIDTH,
    N_BRANCH * D_MODEL,
)
N_IN = sum(SPLITS)

kernel_name = "hybrid_dsa_conformer_gdn_gated"


def rmsnorm(x, g, eps=NORM_EPS):
    xf = x.astype(jnp.float32)
    y = xf * lax.rsqrt(jnp.mean(xf * xf, axis=-1, keepdims=True) + eps)
    return (y * g.astype(jnp.float32)).astype(x.dtype)


def layernorm(x, g, b, eps=LN_EPS):
    xf = x.astype(jnp.float32)
    mu = jnp.mean(xf, axis=-1, keepdims=True)
    var = jnp.mean(jnp.square(xf - mu), axis=-1, keepdims=True)
    y = (xf - mu) * lax.rsqrt(var + eps)
    return (y * g.astype(jnp.float32) + b.astype(jnp.float32)).astype(x.dtype)


def l2norm(x, eps=1e-6):
    xf = x.astype(jnp.float32)
    return (xf * lax.rsqrt(jnp.sum(xf * xf, axis=-1, keepdims=True) + eps)).astype(x.dtype)


def causal_depthwise_conv(x, w):
    k = w.shape[0]
    return lax.conv_general_dilated(
        x, w[:, None, :], window_strides=(1,), padding=[(k - 1, 0)],
        dimension_numbers=("NWC", "WIO", "NWC"), feature_group_count=x.shape[-1])


def alibi_slopes(n):
    return jnp.exp2(-8.0 * jnp.arange(1, n + 1, dtype=jnp.float32) / n)


def dsa_attention(q, c_kv, q_idx, k_idx, idx_w, w_uk, w_uv):
    b, t, h, dh = q.shape
    topk = min(TOPK_MAX, t // 4)
    n_blk = t // Q_BLOCK
    slopes = alibi_slopes(h)
    key_pos = jnp.arange(t, dtype=jnp.int32)

    def block(i):
        start = i * Q_BLOCK
        qpos = start + jnp.arange(Q_BLOCK, dtype=jnp.int32)
        qi = lax.dynamic_slice_in_dim(q_idx, start, Q_BLOCK, axis=1)
        wi = lax.dynamic_slice_in_dim(idx_w, start, Q_BLOCK, axis=1)
        qa = lax.dynamic_slice_in_dim(q, start, Q_BLOCK, axis=1)
        logits = jnp.einsum("bqhd,bsd->bqhs", qi, k_idx).astype(jnp.float32) * (IDX_DIM ** -0.5)
        score = jnp.einsum("bqhs,bqh->bqs", jax.nn.relu(logits), wi.astype(jnp.float32))
        score = jnp.where((key_pos[None, :] <= qpos[:, None])[None], score, -jnp.inf)
        _, idx = lax.top_k(score, topk)
        c_sel = jax.vmap(lambda c, ix: c[ix])(c_kv, idx)
        q_lat = jnp.einsum("bqhd,rhd->bqhr", qa, w_uk) * (dh ** -0.5)
        s = jnp.einsum("bqhr,bqkr->bhqk", q_lat, c_sel).astype(jnp.float32)
        dist = (qpos[None, :, None] - idx).astype(jnp.float32)
        s = s - slopes[None, :, None, None] * dist[:, None]
        s = jnp.where((idx <= qpos[None, :, None])[:, None], s, -jnp.inf)
        p = jax.nn.softmax(s, axis=-1).astype(c_sel.dtype)
        o_lat = jnp.einsum("bhqk,bqkr->bqhr", p, c_sel)
        return jnp.einsum("bqhr,rhd->bqhd", o_lat, w_uv)

    out = lax.map(block, jnp.arange(n_blk, dtype=jnp.int32))
    return jnp.moveaxis(out, 0, 1).reshape(b, t, h * dh)


def conformer_conv(v, gate, conv_w, conv_b, ln_g, ln_b):
    h = v * jax.nn.sigmoid(gate)
    h = causal_depthwise_conv(h, conv_w) + conv_b
    h = layernorm(h, ln_g, ln_b)
    return jax.nn.silu(h)


def gated_delta_rule(q, k, v, g, beta):
    dtype = v.dtype
    b, t, h, dk = q.shape
    dv = v.shape[-1]
    n = t // CHUNK

    def to_chunks(a):
        a = a.astype(jnp.float32).reshape((b, n, CHUNK, h) + a.shape[3:])
        return jnp.moveaxis(jnp.moveaxis(a, 1, 0), 3, 2)

    q, k, v, g, beta = (to_chunks(a) for a in (q, k, v, g, beta))
    q = q * (dk ** -0.5)
    gc = jnp.cumsum(g, axis=-1)
    row = jnp.arange(CHUNK)
    causal = row[:, None] >= row[None, :]
    strict = row[:, None] > row[None, :]
    decay = jnp.exp(jnp.where(causal, gc[..., :, None] - gc[..., None, :], -jnp.inf))
    k_beta = k * beta[..., None]
    m = jnp.where(strict, jnp.einsum("nbhcd,nbhsd->nbhcs", k_beta, k) * decay, 0.0)
    a_mat = m + jnp.eye(CHUNK, dtype=jnp.float32)
    rhs = jnp.concatenate([v * beta[..., None], k_beta * jnp.exp(gc)[..., None]], axis=-1)
    sol = lax.linalg.triangular_solve(a_mat, rhs, left_side=True, lower=True, unit_diagonal=True)
    u, w = sol[..., :dv], sol[..., dv:]
    a_qk = jnp.einsum("nbhcd,nbhsd->nbhcs", q, k) * decay

    def step(state, xs):
        qi, ki, ui, wi, gi, aqi = xs
        v_new = ui - jnp.einsum("bhcd,bhde->bhce", wi, state)
        o = (jnp.einsum("bhcd,bhde->bhce", qi * jnp.exp(gi)[..., None], state)
             + jnp.einsum("bhcs,bhse->bhce", aqi, v_new))
        g_last = gi[..., -1]
        k_dec = ki * jnp.exp(g_last[..., None] - gi)[..., None]
        state = state * jnp.exp(g_last)[..., None, None] + jnp.einsum("bhcd,bhce->bhde", k_dec, v_new)
        return state, o

    state0 = jnp.zeros((b, h, dk, dv), jnp.float32)
    _, o = lax.scan(step, state0, (q, k, u, w, gc, a_qk))
    return jnp.transpose(o, (1, 0, 3, 2, 4)).reshape(b, t, h, dv).astype(dtype)


def gated_deltanet(qkv, beta_logit, alpha_logit, conv_w, a_log, dt_bias):
    b, t, _ = qkv.shape
    qkv = jax.nn.silu(causal_depthwise_conv(qkv, conv_w))
    q, k, v = jnp.split(qkv, 3, axis=-1)
    q = l2norm(q.reshape(b, t, C_HEADS, C_HEAD_DIM))
    k = l2norm(k.reshape(b, t, C_HEADS, C_HEAD_DIM))
    v = v.reshape(b, t, C_HEADS, C_HEAD_DIM)
    beta = jax.nn.sigmoid(beta_logit.astype(jnp.float32))
    g = -jnp.exp(a_log.astype(jnp.float32)) * jax.nn.softplus(
        alpha_logit.astype(jnp.float32) + dt_bias.astype(jnp.float32))
    return gated_delta_rule(q, k, v, g, beta)


def setup_inputs(seed: int = 0) -> dict:
    key = jax.random.key(seed)
    ks = jax.random.split(key, 24)
    f32 = jnp.float32

    def nrm(k, shape, scale):
        return jax.random.normal(k, shape, f32) * scale

    def gain(k, shape):
        return 1.0 + 0.02 * jax.random.normal(k, shape, f32)

    dt = jnp.exp(jax.random.uniform(ks[15], (DEPTH, C_HEADS), f32, math.log(1e-3), math.log(1e-1)))
    dt_bias = dt + jnp.log(-jnp.expm1(-dt))
    a_log = jnp.log(jax.random.uniform(ks[14], (DEPTH, C_HEADS), f32, 1.0, 16.0))
    return {
        "x": jax.random.normal(ks[0], (BATCH, SEQ, D_MODEL), f32),
        "norm_g": gain(ks[1], (DEPTH, D_MODEL)),
        "w_in": nrm(ks[2], (DEPTH, D_MODEL, N_IN), D_MODEL ** -0.5),
        "kv_norm_g": gain(ks[3], (DEPTH, KV_RANK)),
        "idx_k_norm_g": gain(ks[4], (DEPTH, IDX_DIM)),
        "w_uk": nrm(ks[5], (DEPTH, KV_RANK, A_HEADS, A_HEAD_DIM), KV_RANK ** -0.5),
        "w_uv": nrm(ks[6], (DEPTH, KV_RANK, A_HEADS, A_HEAD_DIM), KV_RANK ** -0.5),
        "w_proj_a": nrm(ks[7], (DEPTH, A_WIDTH, D_MODEL), A_WIDTH ** -0.5),
        "conv_w_b": nrm(ks[8], (DEPTH, CONV_K, B_WIDTH), CONV_K ** -0.5),
        "conv_b_b": nrm(ks[9], (DEPTH, B_WIDTH), 0.02),
        "ln_g_b": gain(ks[10], (DEPTH, B_WIDTH)),
        "ln_b_b": nrm(ks[11], (DEPTH, B_WIDTH), 0.02),
        "w_proj_b": nrm(ks[12], (DEPTH, B_WIDTH, D_MODEL), B_WIDTH ** -0.5),
        "conv_w_c": nrm(ks[13], (DEPTH, SHORT_K, 3 * C_WIDTH), SHORT_K ** -0.5),
        "a_log": a_log,
        "dt_bias": dt_bias,
        "onorm_g_c": gain(ks[16], (DEPTH, C_HEAD_DIM)),
        "w_proj_c": nrm(ks[17], (DEPTH, C_WIDTH, D_MODEL), C_WIDTH ** -0.5),
        "w_out": nrm(ks[18], (DEPTH, D_MODEL, D_MODEL), 0.5 * D_MODEL ** -0.5),
        "final_g": gain(ks[19], (D_MODEL,)),
    }


def reference(x, norm_g, w_in, kv_norm_g, idx_k_norm_g, w_uk, w_uv, w_proj_a, conv_w_b, conv_b_b,
              ln_g_b, ln_b_b, w_proj_b, conv_w_c, a_log, dt_bias, onorm_g_c, w_proj_c, w_out, final_g):
    b, t, _ = x.shape
    offsets = np.cumsum(SPLITS)[:-1].tolist()
    for l in range(DEPTH):
        xn = rmsnorm(x, norm_g[l])
        proj = jnp.einsum("btd,de->bte", xn, w_in[l])
        (q_a, c_kv, q_idx, k_idx, idx_w, z_a, glu_v, glu_g, z_b,
         qkv_c, beta_c, alpha_c, z_c, gate_logits) = jnp.split(proj, offsets, axis=-1)

        o_a = dsa_attention(q_a.reshape(b, t, A_HEADS, A_HEAD_DIM),
                            rmsnorm(c_kv, kv_norm_g[l]),
                            q_idx.reshape(b, t, IDX_HEADS, IDX_DIM),
                            rmsnorm(k_idx, idx_k_norm_g[l]),
                            idx_w * (IDX_HEADS ** -0.5), w_uk[l], w_uv[l])
        y_a = jnp.einsum("btc,cd->btd", o_a * jax.nn.silu(z_a), w_proj_a[l])

        h_b = conformer_conv(glu_v, glu_g, conv_w_b[l], conv_b_b[l], ln_g_b[l], ln_b_b[l])
        y_b = jnp.einsum("btc,cd->btd", h_b * jax.nn.silu(z_b), w_proj_b[l])

        o_c = gated_deltanet(qkv_c, beta_c, alpha_c, conv_w_c[l], a_log[l], dt_bias[l])
        o_c = rmsnorm(o_c, onorm_g_c[l]).reshape(b, t, C_WIDTH)
        y_c = jnp.einsum("btc,cd->btd", o_c * jax.nn.silu(z_c), w_proj_c[l])

        gates = jax.nn.sigmoid(gate_logits.astype(jnp.float32)).astype(x.dtype).reshape(b, t, N_BRANCH, D_MODEL)
        merged = gates[:, :, 0] * y_a + gates[:, :, 1] * y_b + gates[:, :, 2] * y_c
        x = x + jnp.einsum("btd,de->bte", merged, w_out[l])
    return rmsnorm(x, final_g)
```

```python
import functools
import math

import jax
import jax.numpy as jnp
import numpy as np
from jax import lax
from jax.experimental import pallas as pl
from jax.experimental.pallas import tpu as pltpu

F32 = jnp.float32
BF16 = jnp.bfloat16

D_MODEL = 2048
NORM_EPS = 1e-6
LN_EPS = 1e-5
L2_EPS = 1e-6
A_HEADS = 8
A_HEAD_DIM = 128
A_WIDTH = A_HEADS * A_HEAD_DIM
KV_RANK = 512
IDX_HEADS = 16
IDX_DIM = 64
TOPK_MAX = 256
B_WIDTH = 1024
CONV_K = 31
C_HEADS = 16
C_HEAD_DIM = 128
C_WIDTH = C_HEADS * C_HEAD_DIM
SHORT_K = 4
CHUNK = 64
N_BRANCH = 3

LANES = 128
SUBLANES = 8
VMEM_LIMIT = 56 * 1024 * 1024

OFF_QA = 0
OFF_QIDX = OFF_QA + A_WIDTH
OFF_ZA = OFF_QIDX + IDX_HEADS * IDX_DIM
OFF_GLUV = OFF_ZA + A_WIDTH
OFF_GLUG = OFF_GLUV + B_WIDTH
OFF_ZB = OFF_GLUG + B_WIDTH
OFF_QKV = OFF_ZB + B_WIDTH
OFF_ZC = OFF_QKV + 3 * C_WIDTH
OFF_GATE = OFF_ZC + C_WIDTH
OFF_CKV = OFF_GATE + N_BRANCH * D_MODEL
P_COLS = OFF_CKV + KV_RANK
SM_KIDX = 0
SM_IDXW = SM_KIDX + IDX_DIM
SM_BETA = SM_IDXW + IDX_HEADS
SM_ALPHA = SM_BETA + C_HEADS
SM_COLS = LANES

INT_MIN = -(2**31)


def _cparams(sem):
    return pltpu.CompilerParams(dimension_semantics=sem, vmem_limit_bytes=VMEM_LIMIT)


def _sigmoid(x):
    return 1.0 / (1.0 + jnp.exp(-x))


def _silu(x):
    return x * _sigmoid(x)


def _dot(a, b):
    return jnp.dot(a, b, preferred_element_type=F32)


def _dot_nt(a, b):
    return lax.dot_general(a, b, (((1,), (1,)), ((), ())), preferred_element_type=F32)


def _dot_tn(a, b):
    return lax.dot_general(a, b, (((0,), (0,)), ((), ())), preferred_element_type=F32)


def _bdot(a, b):
    return lax.dot_general(a, b, (((2,), (1,)), ((0,), (0,))), preferred_element_type=F32)


def _bdot_nt(a, b):
    return lax.dot_general(a, b, (((2,), (2,)), ((0,), (0,))), preferred_element_type=F32)


def _rms_proj_kernel(x_ref, g_ref, w_ref, o_ref, xn_ref):
    @pl.when(pl.program_id(1) == 0)
    def _():
        x = x_ref[...]
        ms = jnp.mean(x * x, axis=-1, keepdims=True)
        xn_ref[...] = (x * lax.rsqrt(ms + NORM_EPS) * g_ref[...]).astype(BF16)

    o_ref[...] = _dot(xn_ref[...], w_ref[...]).astype(o_ref.dtype)


def _rms_proj(x2d, g, w, out_dtype, tm, tn):
    n, d = x2d.shape
    cols = w.shape[1]
    return pl.pallas_call(
        _rms_proj_kernel,
        grid=(n // tm, cols // tn),
        in_specs=[
            pl.BlockSpec((tm, d), lambda i, j: (i, 0)),
            pl.BlockSpec((1, d), lambda i, j: (0, 0)),
            pl.BlockSpec((d, tn), lambda i, j: (0, j)),
        ],
        out_specs=pl.BlockSpec((tm, tn), lambda i, j: (i, j)),
        out_shape=jax.ShapeDtypeStruct((n, cols), out_dtype),
        scratch_shapes=[pltpu.VMEM((tm, d), BF16)],
        compiler_params=_cparams(("parallel", "arbitrary")),
        name="rms_proj",
    )(x2d, g, w)


def _kv_kernel(c_ref, g_ref, wk_ref, wv_ref, k_ref, v_ref):
    c = c_ref[...].astype(F32)
    ms = jnp.mean(c * c, axis=-1, keepdims=True)
    cn = (c * lax.rsqrt(ms + NORM_EPS) * g_ref[...]).astype(BF16)
    k_ref[...] = _dot(cn, wk_ref[...]).astype(BF16)
    v_ref[...] = _dot(cn, wv_ref[...]).astype(BF16)


def _kv_expand(p, g, wk, wv, tm):
    n = p.shape[0]
    return pl.pallas_call(
        _kv_kernel,
        grid=(n // tm,),
        in_specs=[
            pl.BlockSpec((tm, KV_RANK), lambda i: (i, OFF_CKV // KV_RANK)),
            pl.BlockSpec((1, KV_RANK), lambda i: (0, 0)),
            pl.BlockSpec((KV_RANK, A_WIDTH), lambda i: (0, 0)),
            pl.BlockSpec((KV_RANK, A_WIDTH), lambda i: (0, 0)),
        ],
        out_specs=[
            pl.BlockSpec((tm, A_WIDTH), lambda i: (i, 0)),
            pl.BlockSpec((tm, A_WIDTH), lambda i: (i, 0)),
        ],
        out_shape=[jax.ShapeDtypeStruct((n, A_WIDTH), BF16)] * 2,
        compiler_params=_cparams(("parallel",)),
        name="kv_expand",
    )(p, g, wk, wv)


def _dsa_kernel(qa_ref, qi_ref, za_ref, smq_ref, smk_ref, gk_ref, k_ref, v_ref, o_ref,
                kn_ref, sc_ref, *, tq, tk, topk):
    qb = pl.program_id(1)
    t = smk_ref.shape[0]

    @pl.when(qb == 0)
    def _():
        kx = smk_ref[...]
        lane = lax.broadcasted_iota(jnp.int32, (1, SM_COLS), 1)
        is_k = lane < IDX_DIM
        kx = jnp.where(is_k, kx, 0.0)
        ms = jnp.sum(kx * kx, axis=-1, keepdims=True) * (1.0 / IDX_DIM)
        kn = kx * lax.rsqrt(ms + NORM_EPS) * gk_ref[...]
        kn_ref[0] = kn.astype(BF16)
        kn_ref[1] = pltpu.roll(kn, IDX_DIM, 1).astype(BF16)

    nkc = (qb * tq + tq + tk - 1) // tk
    qpos = qb * tq + lax.broadcasted_iota(jnp.int32, (tq, 1), 0)
    w_idx = smq_ref[:, SM_IDXW:SM_IDXW + IDX_HEADS] * (IDX_HEADS ** -0.5)

    def score_chunk(c, carry):
        k0 = pl.multiple_of(c * tk, tk)
        acc = jnp.zeros((tq, tk), F32)
        for hp in range(IDX_HEADS // 2):
            qpair = qi_ref[:, hp * LANES:(hp + 1) * LANES]
            for half in range(2):
                h = 2 * hp + half
                kc = kn_ref[half, pl.ds(k0, tk), :]
                lg = _dot_nt(qpair, kc) * (IDX_DIM ** -0.5)
                acc = acc + w_idx[:, h:h + 1] * jnp.maximum(lg, 0.0)
        bits = pltpu.bitcast(acc, jnp.int32)
        skey = jnp.where(bits < 0, bits ^ jnp.int32(0x7FFFFFFF), bits)
        kpos = k0 + lax.broadcasted_iota(jnp.int32, (1, tk), 1)
        sc_ref[c] = jnp.where(kpos <= qpos, skey, jnp.int32(INT_MIN))
        return carry

    lax.fori_loop(0, nkc, score_chunk, 0)

    def count_ge(cand):
        def body(c, acc):
            ind = jnp.where(sc_ref[c] >= cand, 1.0, 0.0)
            part = ind[:, 0:LANES]
            for g in range(1, tk // LANES):
                part = part + ind[:, g * LANES:(g + 1) * LANES]
            return acc + part

        acc = lax.fori_loop(0, nkc, body, jnp.zeros((tq, LANES), F32))
        return jnp.sum(acc, axis=1, keepdims=True)

    cnt0 = count_ge(jnp.zeros((tq, 1), jnp.int32))
    prefix0 = jnp.where(cnt0 >= topk, jnp.int32(0), jnp.int32(INT_MIN))

    def bit_step(i, prefix):
        cand = prefix | (jnp.int32(1) << (30 - i))
        return jnp.where(count_ge(cand) >= topk, cand, prefix)

    kth = lax.fori_loop(0, 31, bit_step, prefix0)
    thr = jnp.maximum(kth, jnp.int32(INT_MIN + 1))

    scale = A_HEAD_DIM ** -0.5
    for h in range(A_HEADS):
        hs = slice(h * A_HEAD_DIM, (h + 1) * A_HEAD_DIM)
        slope = 2.0 ** (-8.0 * (h + 1) / A_HEADS)
        qh = qa_ref[:, hs]

        def att_chunk(c, carry, qh=qh, hs=hs, slope=slope):
            m, l, acc = carry
            k0 = pl.multiple_of(c * tk, tk)
            kc = k_ref[pl.ds(k0, tk), hs]
            vc = v_ref[pl.ds(k0, tk), hs]
            kpos = k0 + lax.broadcasted_iota(jnp.int32, (1, tk), 1)
            dist = (qpos - kpos).astype(F32)
            s = _dot_nt(qh, kc) * scale - slope * dist
            sel = sc_ref[c] >= thr
            s = jnp.where(sel, s, -1e30)
            m_new = jnp.maximum(m, jnp.max(s, axis=1, keepdims=True))
            p = jnp.where(sel, jnp.exp(s - m_new), 0.0)
            alpha = jnp.exp(m - m_new)
            l = alpha * l + jnp.sum(p, axis=1, keepdims=True)
            acc = alpha * acc + _dot(p.astype(BF16), vc)
            return m_new, l, acc

        init = (jnp.full((tq, 1), -1e30, F32), jnp.zeros((tq, 1), F32),
                jnp.zeros((tq, A_HEAD_DIM), F32))
        _, l, acc = lax.fori_loop(0, nkc, att_chunk, init)
        z = za_ref[:, hs].astype(F32)
        o_ref[:, hs] = (acc / l * _silu(z)).astype(o_ref.dtype)


def _dsa(p3, sm3, kmat, vmat, gk, tq, tk, topk):
    b, t, _ = p3.shape
    kern = functools.partial(_dsa_kernel, tq=tq, tk=tk, topk=topk)
    return pl.pallas_call(
        kern,
        grid=(b, t // tq),
        in_specs=[
            pl.BlockSpec((None, tq, A_WIDTH), lambda i, j: (i, j, OFF_QA // A_WIDTH)),
            pl.BlockSpec((None, tq, A_WIDTH), lambda i, j: (i, j, OFF_QIDX // A_WIDTH)),
            pl.BlockSpec((None, tq, A_WIDTH), lambda i, j: (i, j, OFF_ZA // A_WIDTH)),
            pl.BlockSpec((None, tq, SM_COLS), lambda i, j: (i, j, 0)),
            pl.BlockSpec((None, t, SM_COLS), lambda i, j: (i, 0, 0)),
            pl.BlockSpec((1, SM_COLS), lambda i, j: (0, 0)),
            pl.BlockSpec((None, t, A_WIDTH), lambda i, j: (i, 0, 0)),
            pl.BlockSpec((None, t, A_WIDTH), lambda i, j: (i, 0, 0)),
        ],
        out_specs=pl.BlockSpec((None, tq, A_WIDTH), lambda i, j: (i, j, 0)),
        out_shape=jax.ShapeDtypeStruct((b, t, A_WIDTH), BF16),
        scratch_shapes=[
            pltpu.VMEM((2, t, LANES), BF16),
            pltpu.VMEM((t // tk, tq, tk), jnp.int32),
        ],
        compiler_params=_cparams(("parallel", "arbitrary")),
        name="dsa_attention",
    )(p3, p3, p3, sm3, sm3, gk, kmat, vmat)


CONF_HALO = 32
CONF_ROWS = 64
CONF_WIN = CONF_ROWS + CONF_HALO


def _conf_kernel(v_ref, g_ref, vh_ref, gh_ref, z_ref, cw_ref, cb_ref, lg_ref, lb_ref, o_ref,
                 hp_ref, y_ref, *, tt):
    i = pl.program_id(1)
    hh = vh_ref[...].astype(F32) * _sigmoid(gh_ref[...].astype(F32))
    hp_ref[0:CONF_HALO, :] = jnp.where(i > 0, hh, 0.0)
    hp_ref[CONF_HALO:CONF_HALO + tt, :] = v_ref[...].astype(F32) * _sigmoid(g_ref[...].astype(F32))

    first = CONF_HALO - (CONV_K - 1)
    for ct in range(B_WIDTH // LANES):
        cs = slice(ct * LANES, (ct + 1) * LANES)

        def rows(r, carry, cs=cs):
            base = pl.multiple_of(r * CONF_ROWS, CONF_ROWS)
            win = hp_ref[pl.ds(base, CONF_WIN), cs]
            acc = jnp.zeros((CONF_ROWS, LANES), F32)
            for rem in range(SUBLANES):
                rot = win if rem == 0 else pltpu.roll(win, CONF_WIN - rem, 0)
                for j in range(CONV_K):
                    if (first + j) % SUBLANES == rem:
                        a = (first + j) - rem
                        acc = acc + cw_ref[j:j + 1, cs] * rot[a:a + CONF_ROWS]
            y_ref[pl.ds(base, CONF_ROWS), cs] = acc + cb_ref[:, cs]
            return carry

        lax.fori_loop(0, tt // CONF_ROWS, rows, 0)

    y = y_ref[...]
    mu = jnp.mean(y, axis=-1, keepdims=True)
    yc = y - mu
    var = jnp.mean(yc * yc, axis=-1, keepdims=True)
    yn = yc * lax.rsqrt(var + LN_EPS) * lg_ref[...] + lb_ref[...]
    z = z_ref[...].astype(F32)
    o_ref[...] = (_silu(yn) * _silu(z)).astype(o_ref.dtype)


def _conformer(p3, cw, cb, lg, lb, tt):
    b, t, _ = p3.shape
    hb = tt // CONF_HALO
    kern = functools.partial(_conf_kernel, tt=tt)
    cur = lambda off: pl.BlockSpec((None, tt, B_WIDTH), lambda i, j: (i, j, off // B_WIDTH))
    halo = lambda off: pl.BlockSpec(
        (None, CONF_HALO, B_WIDTH), lambda i, j: (i, jnp.maximum(j * hb - 1, 0), off // B_WIDTH))
    vec = lambda r: pl.BlockSpec((r, B_WIDTH), lambda i, j: (0, 0))
    return pl.pallas_call(
        kern,
        grid=(b, t // tt),
        in_specs=[cur(OFF_GLUV), cur(OFF_GLUG), halo(OFF_GLUV), halo(OFF_GLUG), cur(OFF_ZB),
                  vec(CONF_HALO), vec(1), vec(1), vec(1)],
        out_specs=pl.BlockSpec((None, tt, B_WIDTH), lambda i, j: (i, j, 0)),
        out_shape=jax.ShapeDtypeStruct((b, t, B_WIDTH), BF16),
        scratch_shapes=[pltpu.VMEM((CONF_HALO + tt, B_WIDTH), F32), pltpu.VMEM((tt, B_WIDTH), F32)],
        compiler_params=_cparams(("parallel", "parallel")),
        name="conformer",
    )(p3, p3, p3, p3, p3, cw, cb, lg, lb)


GDN_HALO = 8


def _gdn_pre_kernel(x_ref, xh_ref, cw_ref, q_ref, k_ref, v_ref, *, tt):
    i = pl.program_id(1)
    first = GDN_HALO - (SHORT_K - 1)
    outs = (q_ref, k_ref, v_ref)
    for ct in range(3 * C_HEADS):
        cs = slice(ct * LANES, (ct + 1) * LANES)
        which, head = divmod(ct, C_HEADS)
        halo = jnp.where(i > 0, xh_ref[:, cs].astype(F32), 0.0)
        xp = jnp.concatenate([halo, x_ref[:, cs].astype(F32)], axis=0)
        acc = cw_ref[0:1, cs] * xp[first:first + tt]
        for j in range(1, SHORT_K):
            acc = acc + cw_ref[j:j + 1, cs] * xp[first + j:first + j + tt]
        y = _silu(acc)
        if which < 2:
            y = y * lax.rsqrt(jnp.sum(y * y, axis=-1, keepdims=True) + L2_EPS)
        if which == 0:
            y = y * (C_HEAD_DIM ** -0.5)
        outs[which][:, head * LANES:(head + 1) * LANES] = y.astype(BF16)


def _gdn_pre(p3, cw, tt):
    b, t, _ = p3.shape
    hb = tt // GDN_HALO
    w3 = 3 * C_WIDTH
    kern = functools.partial(_gdn_pre_kernel, tt=tt)
    out = pl.BlockSpec((None, tt, C_WIDTH), lambda i, j: (i, j, 0))
    return pl.pallas_call(
        kern,
        grid=(b, t // tt),
        in_specs=[
            pl.BlockSpec((None, tt, w3), lambda i, j: (i, j, OFF_QKV // w3)),
            pl.BlockSpec((None, GDN_HALO, w3), lambda i, j: (i, jnp.maximum(j * hb - 1, 0), OFF_QKV // w3)),
            pl.BlockSpec((SUBLANES, w3), lambda i, j: (0, 0)),
        ],
        out_specs=[out, out, out],
        out_shape=[jax.ShapeDtypeStruct((b, t, C_WIDTH), BF16)] * 3,
        compiler_params=_cparams(("parallel", "parallel")),
        name="gdn_pre",
    )(p3, p3, cw)


def _gdn_gate_kernel(sm_ref, a_ref, dt_ref, o_ref):
    t = sm_ref.shape[0]
    x = sm_ref[...]
    lane = lax.broadcasted_iota(jnp.int32, (1, SM_COLS), 1)
    beta = _sigmoid(x)
    y = x + dt_ref[...]
    softplus = jnp.maximum(y, 0.0) + jnp.log1p(jnp.exp(-jnp.abs(y)))
    g = -jnp.exp(a_ref[...]) * softplus
    row = lax.broadcasted_iota(jnp.int32, (CHUNK, CHUNK), 0)
    col = lax.broadcasted_iota(jnp.int32, (CHUNK, CHUNK), 1)
    tril = (row >= col).astype(F32)
    is_beta = (lane >= SM_BETA) & (lane < SM_BETA + C_HEADS)
    for c in range(t // CHUNK):
        rs = slice(c * CHUNK, (c + 1) * CHUNK)
        gc = jnp.dot(tril, g[rs], preferred_element_type=F32, precision=lax.Precision.HIGHEST)
        o_ref[rs, :] = jnp.where(is_beta, beta[rs], gc)


def _gdn_gate(sm3, a_vec, dt_vec):
    b, t, _ = sm3.shape
    return pl.pallas_call(
        _gdn_gate_kernel,
        grid=(b,),
        in_specs=[
            pl.BlockSpec((None, t, SM_COLS), lambda i: (i, 0, 0)),
            pl.BlockSpec((1, SM_COLS), lambda i: (0, 0)),
            pl.BlockSpec((1, SM_COLS), lambda i: (0, 0)),
        ],
        out_specs=pl.BlockSpec((None, t, SM_COLS), lambda i: (i, 0, 0)),
        out_shape=jax.ShapeDtypeStruct((b, t, SM_COLS), F32),
        compiler_params=_cparams(("parallel",)),
        name="gdn_gate",
    )(sm3, a_vec, dt_vec)


def _unit_lower_inverse(m, idx_r, idx_c):
    n = m.shape[-1]
    eye = (idx_r == idx_c).astype(F32)
    blk = lambda s: (idx_r // s) == (idx_c // s)
    m0 = jnp.where(blk(8), m, 0.0)
    m0b = m0.astype(BF16)
    a2 = _bdot(m0b, m0b)
    p = eye - m0
    p = p + _bdot(p.astype(BF16), a2.astype(BF16))
    a4 = _bdot(a2.astype(BF16), a2.astype(BF16))
    p = p + _bdot(p.astype(BF16), a4.astype(BF16))
    s = 8
    while s < n:
        lo = jnp.where(blk(2 * s) & jnp.logical_not(blk(s)), m, 0.0)
        pb = p.astype(BF16)
        p = p - _bdot(pb, _bdot(lo.astype(BF16), pb).astype(BF16))
        s *= 2
    return p


def _gdn_kernel(q_ref, k_ref, v_ref, z_ref, bcol_ref, gcol_ref, grow_ref, on_ref, o_ref,
                u_ref, w_ref, qe_ref, kd_ref, aqk_ref, eg_ref, *, hb):
    t = q_ref.shape[0]
    nc = t // CHUNK
    idx_r = lax.broadcasted_iota(jnp.int32, (1, CHUNK, CHUNK), 1)
    idx_c = lax.broadcasted_iota(jnp.int32, (1, CHUNK, CHUNK), 2)
    causal = idx_r >= idx_c
    strict = idx_r > idx_c

    for h in range(hb):
        hs = slice(h * C_HEAD_DIM, (h + 1) * C_HEAD_DIM)
        q = q_ref[:, hs].astype(F32).reshape(nc, CHUNK, C_HEAD_DIM)
        k = k_ref[:, hs].astype(F32).reshape(nc, CHUNK, C_HEAD_DIM)
        v = v_ref[:, hs].astype(F32).reshape(nc, CHUNK, C_HEAD_DIM)
        beta = bcol_ref[:, h:h + 1].reshape(nc, CHUNK, 1)
        gc = gcol_ref[:, h:h + 1].reshape(nc, CHUNK, 1)
        gr = grow_ref[h]
        decay = jnp.exp(jnp.where(causal, gc - gr, -jnp.inf))
        kb = k * beta
        kbf = k.astype(BF16)
        m = jnp.where(strict, _bdot_nt(kb.astype(BF16), kbf) * decay, 0.0)
        tinv = _unit_lower_inverse(m, idx_r, idx_c)
        rhs = jnp.concatenate([v * beta, kb * jnp.exp(gc)], axis=-1)
        sol = _bdot(tinv.astype(BF16), rhs.astype(BF16))
        glast = gc[:, CHUNK - 1:CHUNK, :]
        u_ref[h] = sol[:, :, :C_HEAD_DIM]
        w_ref[h] = sol[:, :, C_HEAD_DIM:].astype(BF16)
        qe_ref[h] = (q * jnp.exp(gc)).astype(BF16)
        kd_ref[h] = (k * jnp.exp(glast - gc)).astype(BF16)
        aqk_ref[h] = (_bdot_nt(q.astype(BF16), kbf) * decay).astype(BF16)
        eg_ref[h] = jnp.broadcast_to(jnp.exp(glast), (nc, 1, C_HEAD_DIM))

    def step(c, states):
        r0 = pl.multiple_of(c * CHUNK, CHUNK)
        new_states = []
        for h in range(hb):
            hs = slice(h * C_HEAD_DIM, (h + 1) * C_HEAD_DIM)
            sb = states[h].astype(BF16)
            vn = u_ref[h, c] - _dot(w_ref[h, c], sb)
            vnb = vn.astype(BF16)
            o = _dot(qe_ref[h, c], sb) + _dot(aqk_ref[h, c], vnb)
            new_states.append(states[h] * eg_ref[h, c] + _dot_tn(kd_ref[h, c], vnb))
            ms = jnp.mean(o * o, axis=-1, keepdims=True)
            on = o * lax.rsqrt(ms + NORM_EPS) * on_ref[...]
            z = z_ref[pl.ds(r0, CHUNK), hs].astype(F32)
            o_ref[pl.ds(r0, CHUNK), hs] = (on * _silu(z)).astype(o_ref.dtype)
        return tuple(new_states)

    init = tuple(jnp.zeros((C_HEAD_DIM, C_HEAD_DIM), F32) for _ in range(hb))
    lax.fori_loop(0, nc, step, init)


def _gdn(q3, k3, v3, p3, bcol, gcol, grow, on_g, hb):
    b, t, _ = q3.shape
    nc = t // CHUNK
    wd = hb * C_HEAD_DIM
    kern = functools.partial(_gdn_kernel, hb=hb)
    qkv = pl.BlockSpec((None, t, wd), lambda i, j: (i, 0, j))
    return pl.pallas_call(
        kern,
        grid=(b, C_HEADS // hb),
        in_specs=[
            qkv, qkv, qkv,
            pl.BlockSpec((None, t, wd), lambda i, j: (i, 0, OFF_ZC // wd + j)),
            pl.BlockSpec((None, None, t, hb), lambda i, j: (i, j, 0, 0)),
            pl.BlockSpec((None, None, t, hb), lambda i, j: (i, j, 0, 0)),
            pl.BlockSpec((None, hb, nc, 1, CHUNK), lambda i, j: (i, j, 0, 0, 0)),
            pl.BlockSpec((1, C_HEAD_DIM), lambda i, j: (0, 0)),
        ],
        out_specs=pl.BlockSpec((None, t, wd), lambda i, j: (i, 0, j)),
        out_shape=jax.ShapeDtypeStruct((b, t, C_WIDTH), BF16),
        scratch_shapes=[
            pltpu.VMEM((hb, nc, CHUNK, C_HEAD_DIM), F32),
            pltpu.VMEM((hb, nc, CHUNK, C_HEAD_DIM), BF16),
            pltpu.VMEM((hb, nc, CHUNK, C_HEAD_DIM), BF16),
            pltpu.VMEM((hb, nc, CHUNK, C_HEAD_DIM), BF16),
            pltpu.VMEM((hb, nc, CHUNK, CHUNK), BF16),
            pltpu.VMEM((hb, nc, 1, C_HEAD_DIM), F32),
        ],
        compiler_params=_cparams(("parallel", "parallel")),
        name="gdn_delta_rule",
    )(q3, k3, v3, p3, bcol, gcol, grow, on_g)


def _merge_kernel(ua_ref, ub_ref, uc_ref, g0_ref, g1_ref, g2_ref, wa_ref, wb_ref, wc_ref, o_ref):
    ya = _dot(ua_ref[...], wa_ref[...])
    yb = _dot(ub_ref[...], wb_ref[...])
    yc = _dot(uc_ref[...], wc_ref[...])
    o = (_sigmoid(g0_ref[...].astype(F32)) * ya + _sigmoid(g1_ref[...].astype(F32)) * yb
         + _sigmoid(g2_ref[...].astype(F32)) * yc)
    o_ref[...] = o.astype(o_ref.dtype)


def _merge(ua, ub, uc, p, wa, wb, wc, tm, tn):
    n = ua.shape[0]
    gate = lambda br: pl.BlockSpec((tm, tn), lambda i, j: (i, (OFF_GATE + br * D_MODEL) // tn + j))
    return pl.pallas_call(
        _merge_kernel,
        grid=(n // tm, D_MODEL // tn),
        in_specs=[
            pl.BlockSpec((tm, A_WIDTH), lambda i, j: (i, 0)),
            pl.BlockSpec((tm, B_WIDTH), lambda i, j: (i, 0)),
            pl.BlockSpec((tm, C_WIDTH), lambda i, j: (i, 0)),
            gate(0), gate(1), gate(2),
            pl.BlockSpec((A_WIDTH, tn), lambda i, j: (0, j)),
            pl.BlockSpec((B_WIDTH, tn), lambda i, j: (0, j)),
            pl.BlockSpec((C_WIDTH, tn), lambda i, j: (0, j)),
        ],
        out_specs=pl.BlockSpec((tm, tn), lambda i, j: (i, j)),
        out_shape=jax.ShapeDtypeStruct((n, D_MODEL), BF16),
        compiler_params=_cparams(("parallel", "arbitrary")),
        name="gate_merge",
    )(ua, ub, uc, p, p, p, wa, wb, wc)


def _out_kernel(x_ref, m_ref, w_ref, fg_ref, o_ref, *, final):
    y = x_ref[...] + _dot(m_ref[...], w_ref[...])
    if final:
        ms = jnp.mean(y * y, axis=-1, keepdims=True)
        y = y * lax.rsqrt(ms + NORM_EPS) * fg_ref[...]
    o_ref[...] = y


def _out_proj(x2d, merged, w, fg, tm, final):
    n, d = x2d.shape
    kern = functools.partial(_out_kernel, final=final)
    return pl.pallas_call(
        kern,
        grid=(n // tm,),
        in_specs=[
            pl.BlockSpec((tm, d), lambda i: (i, 0)),
            pl.BlockSpec((tm, d), lambda i: (i, 0)),
            pl.BlockSpec((d, d), lambda i: (0, 0)),
            pl.BlockSpec((1, d), lambda i: (0, 0)),
        ],
        out_specs=pl.BlockSpec((tm, d), lambda i: (i, 0)),
        out_shape=jax.ShapeDtypeStruct((n, d), F32),
        compiler_params=_cparams(("parallel",)),
        name="out_proj",
    )(x2d, merged, w, fg)


_SPLITS = (A_WIDTH, KV_RANK, IDX_HEADS * IDX_DIM, IDX_DIM, IDX_HEADS, A_WIDTH, B_WIDTH, B_WIDTH,
           B_WIDTH, 3 * C_WIDTH, C_HEADS, C_HEADS, C_WIDTH, N_BRANCH * D_MODEL)


def _repack_w_in(w):
    offs = np.concatenate([[0], np.cumsum(_SPLITS)])
    seg = [w[:, offs[i]:offs[i + 1]] for i in range(len(_SPLITS))]
    (q_a, c_kv, q_idx, k_idx, idx_w, z_a, glu_v, glu_g, z_b, qkv, beta, alpha, z_c, gates) = seg
    big = jnp.concatenate([q_a, q_idx, z_a, glu_v, glu_g, z_b, qkv, z_c, gates, c_kv], axis=1)
    pad = jnp.zeros((w.shape[0], SM_COLS - (SM_ALPHA + C_HEADS)), w.dtype)
    small = jnp.concatenate([k_idx, idx_w, beta, alpha, pad], axis=1)
    return big.astype(BF16), small.astype(BF16)


def _pad_lanes(vec, off):
    out = jnp.zeros((1, SM_COLS), F32)
    return out.at[0, off:off + vec.shape[0]].set(vec.astype(F32))


def _pick(total, pref):
    for c in pref:
        if total % c == 0:
            return c
    return total


def _tiles(n, t):
    return dict(tm_proj=_pick(n, (1024, 512, 256)), tq=_pick(t, (256, 128)),
                tk=_pick(t, (512, 256, 128)), tt=_pick(t, (512, 256, 128)),
                tm=_pick(n, (512, 256)), hb=4)


def kernel(x, norm_g, w_in, kv_norm_g, idx_k_norm_g, w_uk, w_uv, w_proj_a, conv_w_b, conv_b_b,
           ln_g_b, ln_b_b, w_proj_b, conv_w_c, a_log, dt_bias, onorm_g_c, w_proj_c, w_out, final_g):
    b, t, _ = x.shape
    return _forward(_tiles(b * t, t), x, norm_g, w_in, kv_norm_g, idx_k_norm_g, w_uk, w_uv, w_proj_a,
                    conv_w_b, conv_b_b, ln_g_b, ln_b_b, w_proj_b, conv_w_c, a_log, dt_bias,
                    onorm_g_c, w_proj_c, w_out, final_g)


def _forward(tiles, x, norm_g, w_in, kv_norm_g, idx_k_norm_g, w_uk, w_uv, w_proj_a, conv_w_b,
             conv_b_b, ln_g_b, ln_b_b, w_proj_b, conv_w_c, a_log, dt_bias, onorm_g_c, w_proj_c,
             w_out, final_g):
    b, t, d = x.shape
    n = b * t
    depth = norm_g.shape[0]
    topk = min(TOPK_MAX, t // 4)
    nc = t // CHUNK
    tm_proj, tq, tk, tt, tm, hb = (tiles[k] for k in ("tm_proj", "tq", "tk", "tt", "tm", "hb"))

    x2d = x.reshape(n, d)
    for l in range(depth):
        w_big, w_small = _repack_w_in(w_in[l])
        p = _rms_proj(x2d, norm_g[l][None, :], w_big, BF16, tm_proj, 512)
        sm = _rms_proj(x2d, norm_g[l][None, :], w_small, F32, tm_proj, SM_COLS)
        p3 = p.reshape(b, t, P_COLS)
        sm3 = sm.reshape(b, t, SM_COLS)

        kmat, vmat = _kv_expand(p, kv_norm_g[l][None, :].astype(F32),
                                w_uk[l].reshape(KV_RANK, A_WIDTH).astype(BF16),
                                w_uv[l].reshape(KV_RANK, A_WIDTH).astype(BF16), tm)
        u_a = _dsa(p3, sm3, kmat.reshape(b, t, A_WIDTH), vmat.reshape(b, t, A_WIDTH),
                   _pad_lanes(idx_k_norm_g[l], SM_KIDX), tq, tk, topk)

        cw_b = jnp.concatenate(
            [conv_w_b[l].astype(F32), jnp.zeros((CONF_HALO - CONV_K, B_WIDTH), F32)], axis=0)
        u_b = _conformer(p3, cw_b, conv_b_b[l][None, :].astype(F32), ln_g_b[l][None, :].astype(F32),
                         ln_b_b[l][None, :].astype(F32), tt)

        cw_c = jnp.concatenate(
            [conv_w_c[l].astype(F32), jnp.zeros((SUBLANES - SHORT_K, 3 * C_WIDTH), F32)], axis=0)
        q3, k3, v3 = _gdn_pre(p3, cw_c, tt)
        gate = _gdn_gate(sm3, _pad_lanes(a_log[l], SM_ALPHA), _pad_lanes(dt_bias[l], SM_ALPHA))
        beta = jnp.transpose(gate[:, :, SM_BETA:SM_BETA + C_HEADS], (0, 2, 1))
        gcum = jnp.transpose(gate[:, :, SM_ALPHA:SM_ALPHA + C_HEADS], (0, 2, 1))
        col = lambda a: jnp.transpose(a.reshape(b, C_HEADS // hb, hb, t), (0, 1, 3, 2))
        u_c = _gdn(q3, k3, v3, p3, col(beta), col(gcum),
                   gcum.reshape(b, C_HEADS, nc, 1, CHUNK), onorm_g_c[l][None, :].astype(F32), hb)

        merged = _merge(u_a.reshape(n, A_WIDTH), u_b.reshape(n, B_WIDTH), u_c.reshape(n, C_WIDTH), p,
                        w_proj_a[l].astype(BF16), w_proj_b[l].astype(BF16), w_proj_c[l].astype(BF16),
                        tm, 512)
        x2d = _out_proj(x2d, merged, w_out[l].astype(BF16), final_g[None, :].astype(F32), tm,
                        l == depth - 1)
    return x2d.reshape(b, t, d)
```

```python
import functools
import math

import jax
import jax.numpy as jnp
import numpy as np
from jax import lax
from jax.experimental import pallas as pl
from jax.experimental.pallas import tpu as pltpu

F32 = jnp.float32
BF16 = jnp.bfloat16

D_MODEL = 2048
NORM_EPS = 1e-6
LN_EPS = 1e-5
L2_EPS = 1e-6
A_HEADS = 8
A_HEAD_DIM = 128
A_WIDTH = A_HEADS * A_HEAD_DIM
KV_RANK = 512
IDX_HEADS = 16
IDX_DIM = 64
TOPK_MAX = 256
B_WIDTH = 1024
CONV_K = 31
C_HEADS = 16
C_HEAD_DIM = 128
C_WIDTH = C_HEADS * C_HEAD_DIM
SHORT_K = 4
CHUNK = 64
N_BRANCH = 3

LANES = 128
SUBLANES = 8
VMEM_LIMIT = 56 * 1024 * 1024

OFF_QA = 0
OFF_QIDX = OFF_QA + A_WIDTH
OFF_ZA = OFF_QIDX + IDX_HEADS * IDX_DIM
OFF_GLUV = OFF_ZA + A_WIDTH
OFF_GLUG = OFF_GLUV + B_WIDTH
OFF_ZB = OFF_GLUG + B_WIDTH
OFF_QKV = OFF_ZB + B_WIDTH
OFF_ZC = OFF_QKV + 3 * C_WIDTH
OFF_GATE = OFF_ZC + C_WIDTH
OFF_CKV = OFF_GATE + N_BRANCH * D_MODEL
P_COLS = OFF_CKV + KV_RANK
SM_KIDX = 0
SM_IDXW = SM_KIDX + IDX_DIM
SM_BETA = SM_IDXW + IDX_HEADS
SM_ALPHA = SM_BETA + C_HEADS
SM_COLS = LANES

INT_MIN = -(2**31)


def _cparams(sem):
    return pltpu.CompilerParams(dimension_semantics=sem, vmem_limit_bytes=VMEM_LIMIT)


def _sigmoid(x):
    return 1.0 / (1.0 + jnp.exp(-x))


def _silu(x):
    return x * _sigmoid(x)


def _dot(a, b):
    return jnp.dot(a, b, preferred_element_type=F32)


def _dot_nt(a, b):
    return lax.dot_general(a, b, (((1,), (1,)), ((), ())), preferred_element_type=F32)


def _dot_tn(a, b):
    return lax.dot_general(a, b, (((0,), (0,)), ((), ())), preferred_element_type=F32)


def _bdot(a, b):
    return lax.dot_general(a, b, (((2,), (1,)), ((0,), (0,))), preferred_element_type=F32)


def _bdot_tn(a, b):
    return lax.dot_general(a, b, (((1,), (1,)), ((0,), (0,))), preferred_element_type=F32)


def _bdot_nt(a, b):
    return lax.dot_general(a, b, (((2,), (2,)), ((0,), (0,))), preferred_element_type=F32)


def _rms_proj_kernel(x_ref, g_ref, w_ref, o_ref, xn_ref):
    @pl.when(pl.program_id(1) == 0)
    def _():
        x = x_ref[...]
        ms = jnp.mean(x * x, axis=-1, keepdims=True)
        xn_ref[...] = (x * lax.rsqrt(ms + NORM_EPS) * g_ref[...]).astype(BF16)

    o_ref[...] = _dot(xn_ref[...], w_ref[...]).astype(o_ref.dtype)


def _rms_proj(x2d, g, w, out_dtype, tm, tn):
    n, d = x2d.shape
    cols = w.shape[1]
    return pl.pallas_call(
        _rms_proj_kernel,
        grid=(n // tm, cols // tn),
        in_specs=[
            pl.BlockSpec((tm, d), lambda i, j: (i, 0)),
            pl.BlockSpec((1, d), lambda i, j: (0, 0)),
            pl.BlockSpec((d, tn), lambda i, j: (0, j)),
        ],
        out_specs=pl.BlockSpec((tm, tn), lambda i, j: (i, j)),
        out_shape=jax.ShapeDtypeStruct((n, cols), out_dtype),
        scratch_shapes=[pltpu.VMEM((tm, d), BF16)],
        compiler_params=_cparams(("parallel", "arbitrary")),
        name="rms_proj",
    )(x2d, g, w)


def _kv_kernel(c_ref, g_ref, wk_ref, wvt_ref, k_ref, vt_ref):
    c = c_ref[...].astype(F32)
    ms = jnp.mean(c * c, axis=-1, keepdims=True)
    cn = (c * lax.rsqrt(ms + NORM_EPS) * g_ref[...]).astype(BF16)
    k_ref[...] = _dot(cn, wk_ref[...]).astype(BF16)
    vt_ref[...] = _dot_nt(wvt_ref[...], cn).astype(BF16)


def _kv_expand(p3, g, wk, wvt, tk):
    b, t, _ = p3.shape
    return pl.pallas_call(
        _kv_kernel,
        grid=(b, t // tk),
        in_specs=[
            pl.BlockSpec((None, tk, KV_RANK), lambda i, j: (i, j, OFF_CKV // KV_RANK)),
            pl.BlockSpec((1, KV_RANK), lambda i, j: (0, 0)),
            pl.BlockSpec((KV_RANK, A_WIDTH), lambda i, j: (0, 0)),
            pl.BlockSpec((A_WIDTH, KV_RANK), lambda i, j: (0, 0)),
        ],
        out_specs=[
            pl.BlockSpec((None, tk, A_WIDTH), lambda i, j: (i, j, 0)),
            pl.BlockSpec((None, None, A_WIDTH, tk), lambda i, j: (i, j, 0, 0)),
        ],
        out_shape=[jax.ShapeDtypeStruct((b, t, A_WIDTH), BF16),
                   jax.ShapeDtypeStruct((b, t // tk, A_WIDTH, tk), BF16)],
        compiler_params=_cparams(("parallel", "parallel")),
        name="kv_expand",
    )(p3, g, wk, wvt)


LOG2E = 1.4426950408889634
MASK_BIAS = -1e30
M_INIT = -1e29
CNT_ROWS = 32
SEL_UNROLL = 4


def _dsa_kernel(qa_ref, qi_ref, za_ref, smq_ref, smk_ref, gk_ref, k_ref, vt_ref, o_ref,
                kn_ref, sc_ref, qs_ref, m_ref, l_ref, acc_ref, s_ref, *, tq, tk, topk):
    qb = pl.program_id(1)

    @pl.when(qb == 0)
    def _():
        kx = smk_ref[...]
        lane = lax.broadcasted_iota(jnp.int32, (1, SM_COLS), 1)
        is_k = lane < IDX_DIM
        kx = jnp.where(is_k, kx, 0.0)
        ms = jnp.sum(kx * kx, axis=-1, keepdims=True) * (1.0 / IDX_DIM)
        kn = kx * lax.rsqrt(ms + NORM_EPS) * gk_ref[...]
        kn_ref[0] = kn.astype(BF16)
        kn_ref[1] = pltpu.roll(kn, IDX_DIM, 1).astype(BF16)

    nkc = (qb * tq + tq + tk - 1) // tk
    qpos = qb * tq + lax.broadcasted_iota(jnp.int32, (1, tq), 1)
    kpos0 = lax.broadcasted_iota(jnp.int32, (tk, 1), 0)
    w_t = smq_ref[...].T * (IDX_HEADS ** -0.5 * IDX_DIM ** -0.5)

    def score_chunk(c, carry):
        k0 = pl.multiple_of(c * tk, tk)
        acc = jnp.zeros((tk, tq), F32)
        for hp in range(IDX_HEADS // 2):
            qpair = qi_ref[:, hp * LANES:(hp + 1) * LANES]
            for half in range(2):
                row = SM_IDXW + 2 * hp + half
                kc = kn_ref[half, pl.ds(k0, tk), :]
                acc = acc + w_t[row:row + 1, :] * jnp.maximum(_dot_nt(kc, qpair), 0.0)
        bits = pltpu.bitcast(acc, jnp.int32)
        skey = jnp.where(bits < 0, bits ^ jnp.int32(0x7FFFFFFF), bits)
        sc_ref[c] = jnp.where(k0 + kpos0 <= qpos, skey, jnp.int32(INT_MIN))
        return carry

    lax.fori_loop(0, nkc, score_chunk, 0)

    def count_ge(cand):
        def body(c, acc):
            ind = jnp.where(sc_ref[c] >= cand, 1.0, 0.0)
            return acc + jnp.sum(ind.reshape(tk // CNT_ROWS, CNT_ROWS, tq), axis=0)

        acc = lax.fori_loop(0, nkc, body, jnp.zeros((CNT_ROWS, tq), F32))
        return jnp.sum(acc, axis=0, keepdims=True)

    fkeep = jnp.float32(topk)

    def sel_cond(st):
        i, _, done = st
        return (i < 32) & (jnp.min(done) < 0.5)

    def sel_step(st):
        i, prefix, done = st
        for u in range(SEL_UNROLL):
            cand = prefix ^ (jnp.int32(1) << (31 - u - i))
            cnt = count_ge(cand)
            prefix = jnp.where((done < 0.5) & (cnt >= fkeep), cand, prefix)
            done = jnp.where(cnt == fkeep, 1.0, done)
        return i + SEL_UNROLL, prefix, done

    done0 = jnp.where(qpos < topk, 1.0, 0.0)
    _, kth, _ = lax.while_loop(
        sel_cond, sel_step, (jnp.int32(0), jnp.full((1, tq), INT_MIN, jnp.int32), done0))
    thr = jnp.maximum(kth, jnp.int32(INT_MIN + 1))

    qs_ref[...] = (qa_ref[...].astype(F32) * (A_HEAD_DIM ** -0.5 * LOG2E)).astype(BF16)
    m_ref[...] = jnp.full(m_ref.shape, M_INIT, F32)
    l_ref[...] = jnp.zeros(l_ref.shape, F32)
    acc_ref[...] = jnp.zeros(acc_ref.shape, F32)

    def qk_scores(c, slot):
        k0 = pl.multiple_of(c * tk, tk)
        for h in range(A_HEADS):
            hs = slice(h * A_HEAD_DIM, (h + 1) * A_HEAD_DIM)
            s_ref[slot, h] = _dot_nt(k_ref[pl.ds(k0, tk), hs], qs_ref[:, hs])

    def att_chunk(c, carry):
        slot = c % 2
        ndist = (c * tk + kpos0 - qpos).astype(F32) * LOG2E
        bias = jnp.where(sc_ref[c] >= thr, 0.0, MASK_BIAS)
        for h in range(A_HEADS):
            hs = slice(h * A_HEAD_DIM, (h + 1) * A_HEAD_DIM)
            slope = 2.0 ** (-8.0 * (h + 1) / A_HEADS)
            s = s_ref[slot, h] + (slope * ndist + bias)
            m_old = m_ref[h]
            m_new = jnp.maximum(m_old, jnp.max(s, axis=0, keepdims=True))
            p = jnp.exp2(s - m_new)
            alpha = jnp.exp2(m_old - m_new)
            l_ref[h] = alpha * l_ref[h] + jnp.sum(p, axis=0, keepdims=True)
            acc_ref[hs, :] = alpha * acc_ref[hs, :] + _dot(vt_ref[c, hs, :], p.astype(BF16))
            m_ref[h] = m_new
        qk_scores(jnp.minimum(c + 1, nkc - 1), 1 - slot)
        return carry

    qk_scores(0, 0)
    lax.fori_loop(0, nkc, att_chunk, 0)
    for h in range(A_HEADS):
        hs = slice(h * A_HEAD_DIM, (h + 1) * A_HEAD_DIM)
        z = za_ref[:, hs].astype(F32)
        o_ref[:, hs] = ((acc_ref[hs, :] / l_ref[h]).T * _silu(z)).astype(o_ref.dtype)


def _dsa(p3, sm3, kmat, vmat_t, gk, tq, tk, topk):
    b, t, _ = p3.shape
    kern = functools.partial(_dsa_kernel, tq=tq, tk=tk, topk=topk)
    return pl.pallas_call(
        kern,
        grid=(b, t // tq),
        in_specs=[
            pl.BlockSpec((None, tq, A_WIDTH), lambda i, j: (i, j, OFF_QA // A_WIDTH)),
            pl.BlockSpec((None, tq, A_WIDTH), lambda i, j: (i, j, OFF_QIDX // A_WIDTH)),
            pl.BlockSpec((None, tq, A_WIDTH), lambda i, j: (i, j, OFF_ZA // A_WIDTH)),
            pl.BlockSpec((None, tq, SM_COLS), lambda i, j: (i, j, 0)),
            pl.BlockSpec((None, t, SM_COLS), lambda i, j: (i, 0, 0)),
            pl.BlockSpec((1, SM_COLS), lambda i, j: (0, 0)),
            pl.BlockSpec((None, t, A_WIDTH), lambda i, j: (i, 0, 0)),
            pl.BlockSpec((None, t // tk, A_WIDTH, tk), lambda i, j: (i, 0, 0, 0)),
        ],
        out_specs=pl.BlockSpec((None, tq, A_WIDTH), lambda i, j: (i, j, 0)),
        out_shape=jax.ShapeDtypeStruct((b, t, A_WIDTH), BF16),
        scratch_shapes=[
            pltpu.VMEM((2, t, LANES), BF16),
            pltpu.VMEM((t // tk, tk, tq), jnp.int32),
            pltpu.VMEM((tq, A_WIDTH), BF16),
            pltpu.VMEM((A_HEADS, 1, tq), F32),
            pltpu.VMEM((A_HEADS, 1, tq), F32),
            pltpu.VMEM((A_WIDTH, tq), F32),
            pltpu.VMEM((2, A_HEADS, tk, tq), F32),
        ],
        compiler_params=_cparams(("parallel", "arbitrary")),
        name="dsa_attention",
    )(p3, p3, p3, sm3, sm3, gk, kmat, vmat_t)


CONF_HALO = 32
CONF_ROWS = 64
CONF_WIN = CONF_ROWS + CONF_HALO


def _conf_kernel(v_ref, g_ref, vh_ref, gh_ref, z_ref, cw_ref, cb_ref, lg_ref, lb_ref, o_ref,
                 hp_ref, y_ref, *, tt):
    i = pl.program_id(1)
    hh = vh_ref[...].astype(F32) * _sigmoid(gh_ref[...].astype(F32))
    hp_ref[0:CONF_HALO, :] = jnp.where(i > 0, hh, 0.0)
    hp_ref[CONF_HALO:CONF_HALO + tt, :] = v_ref[...].astype(F32) * _sigmoid(g_ref[...].astype(F32))

    first = CONF_HALO - (CONV_K - 1)
    for ct in range(B_WIDTH // LANES):
        cs = slice(ct * LANES, (ct + 1) * LANES)

        def rows(r, carry, cs=cs):
            base = pl.multiple_of(r * CONF_ROWS, CONF_ROWS)
            win = hp_ref[pl.ds(base, CONF_WIN), cs]
            acc = jnp.zeros((CONF_ROWS, LANES), F32)
            for rem in range(SUBLANES):
                rot = win if rem == 0 else pltpu.roll(win, CONF_WIN - rem, 0)
                for j in range(CONV_K):
                    if (first + j) % SUBLANES == rem:
                        a = (first + j) - rem
                        acc = acc + cw_ref[j:j + 1, cs] * rot[a:a + CONF_ROWS]
            y_ref[pl.ds(base, CONF_ROWS), cs] = acc + cb_ref[:, cs]
            return carry

        lax.fori_loop(0, tt // CONF_ROWS, rows, 0)

    y = y_ref[...]
    mu = jnp.mean(y, axis=-1, keepdims=True)
    yc = y - mu
    var = jnp.mean(yc * yc, axis=-1, keepdims=True)
    yn = yc * lax.rsqrt(var + LN_EPS) * lg_ref[...] + lb_ref[...]
    z = z_ref[...].astype(F32)
    o_ref[...] = (_silu(yn) * _silu(z)).astype(o_ref.dtype)


def _conformer(p3, cw, cb, lg, lb, tt):
    b, t, _ = p3.shape
    hb = tt // CONF_HALO
    kern = functools.partial(_conf_kernel, tt=tt)
    cur = lambda off: pl.BlockSpec((None, tt, B_WIDTH), lambda i, j: (i, j, off // B_WIDTH))
    halo = lambda off: pl.BlockSpec(
        (None, CONF_HALO, B_WIDTH), lambda i, j: (i, jnp.maximum(j * hb - 1, 0), off // B_WIDTH))
    vec = lambda r: pl.BlockSpec((r, B_WIDTH), lambda i, j: (0, 0))
    return pl.pallas_call(
        kern,
        grid=(b, t // tt),
        in_specs=[cur(OFF_GLUV), cur(OFF_GLUG), halo(OFF_GLUV), halo(OFF_GLUG), cur(OFF_ZB),
                  vec(CONF_HALO), vec(1), vec(1), vec(1)],
        out_specs=pl.BlockSpec((None, tt, B_WIDTH), lambda i, j: (i, j, 0)),
        out_shape=jax.ShapeDtypeStruct((b, t, B_WIDTH), BF16),
        scratch_shapes=[pltpu.VMEM((CONF_HALO + tt, B_WIDTH), F32), pltpu.VMEM((tt, B_WIDTH), F32)],
        compiler_params=_cparams(("parallel", "parallel")),
        name="conformer",
    )(p3, p3, p3, p3, p3, cw, cb, lg, lb)


GDN_HALO = 8


def _gdn_pre_kernel(x_ref, xh_ref, cw_ref, q_ref, k_ref, v_ref, *, tt):
    i = pl.program_id(1)
    first = GDN_HALO - (SHORT_K - 1)
    outs = (q_ref, k_ref, v_ref)
    for ct in range(3 * C_HEADS):
        cs = slice(ct * LANES, (ct + 1) * LANES)
        which, head = divmod(ct, C_HEADS)
        halo = jnp.where(i > 0, xh_ref[:, cs].astype(F32), 0.0)
        xp = jnp.concatenate([halo, x_ref[:, cs].astype(F32)], axis=0)
        acc = cw_ref[0:1, cs] * xp[first:first + tt]
        for j in range(1, SHORT_K):
            acc = acc + cw_ref[j:j + 1, cs] * xp[first + j:first + j + tt]
        y = _silu(acc)
        if which < 2:
            y = y * lax.rsqrt(jnp.sum(y * y, axis=-1, keepdims=True) + L2_EPS)
        if which == 0:
            y = y * (C_HEAD_DIM ** -0.5)
        outs[which][:, head * LANES:(head + 1) * LANES] = y.astype(BF16)


def _gdn_pre(p3, cw, tt):
    b, t, _ = p3.shape
    hb = tt // GDN_HALO
    w3 = 3 * C_WIDTH
    kern = functools.partial(_gdn_pre_kernel, tt=tt)
    out = pl.BlockSpec((None, tt, C_WIDTH), lambda i, j: (i, j, 0))
    return pl.pallas_call(
        kern,
        grid=(b, t // tt),
        in_specs=[
            pl.BlockSpec((None, tt, w3), lambda i, j: (i, j, OFF_QKV // w3)),
            pl.BlockSpec((None, GDN_HALO, w3), lambda i, j: (i, jnp.maximum(j * hb - 1, 0), OFF_QKV // w3)),
            pl.BlockSpec((SUBLANES, w3), lambda i, j: (0, 0)),
        ],
        out_specs=[out, out, out],
        out_shape=[jax.ShapeDtypeStruct((b, t, C_WIDTH), BF16)] * 3,
        compiler_params=_cparams(("parallel", "parallel")),
        name="gdn_pre",
    )(p3, p3, cw)


def _gdn_gate_kernel(sm_ref, a_ref, dt_ref, o_ref):
    t = sm_ref.shape[0]
    x = sm_ref[...]
    lane = lax.broadcasted_iota(jnp.int32, (1, SM_COLS), 1)
    beta = _sigmoid(x)
    y = x + dt_ref[...]
    softplus = jnp.maximum(y, 0.0) + jnp.log1p(jnp.exp(-jnp.abs(y)))
    g = -jnp.exp(a_ref[...]) * softplus
    row = lax.broadcasted_iota(jnp.int32, (CHUNK, CHUNK), 0)
    col = lax.broadcasted_iota(jnp.int32, (CHUNK, CHUNK), 1)
    tril = (row >= col).astype(F32)
    is_beta = (lane >= SM_BETA) & (lane < SM_BETA + C_HEADS)
    for c in range(t // CHUNK):
        rs = slice(c * CHUNK, (c + 1) * CHUNK)
        gc = jnp.dot(tril, g[rs], preferred_element_type=F32, precision=lax.Precision.HIGHEST)
        o_ref[rs, :] = jnp.where(is_beta, beta[rs], gc)


def _gdn_gate(sm3, a_vec, dt_vec):
    b, t, _ = sm3.shape
    return pl.pallas_call(
        _gdn_gate_kernel,
        grid=(b,),
        in_specs=[
            pl.BlockSpec((None, t, SM_COLS), lambda i: (i, 0, 0)),
            pl.BlockSpec((1, SM_COLS), lambda i: (0, 0)),
            pl.BlockSpec((1, SM_COLS), lambda i: (0, 0)),
        ],
        out_specs=pl.BlockSpec((None, t, SM_COLS), lambda i: (i, 0, 0)),
        out_shape=jax.ShapeDtypeStruct((b, t, SM_COLS), F32),
        compiler_params=_cparams(("parallel",)),
        name="gdn_gate",
    )(sm3, a_vec, dt_vec)


GDN_BATCH = 32


def _unit_lower_inverse(m, idx_r, idx_c):
    n = m.shape[-1]
    eye = (idx_r == idx_c).astype(F32)
    blk = lambda s: (idx_r // s) == (idx_c // s)
    m0 = jnp.where(blk(8), m, 0.0)
    m0b = m0.astype(BF16)
    a2 = _bdot(m0b, m0b)
    p = eye - m0
    p = p + _bdot(p.astype(BF16), a2.astype(BF16))
    a4 = _bdot(a2.astype(BF16), a2.astype(BF16))
    p = p + _bdot(p.astype(BF16), a4.astype(BF16))
    s = 8
    while s < n:
        lo = jnp.where(blk(2 * s) & jnp.logical_not(blk(s)), m, 0.0)
        pb = p.astype(BF16)
        p = p - _bdot(pb, _bdot(lo.astype(BF16), pb).astype(BF16))
        s *= 2
    return p


def _gdn_kernel(q_ref, k_ref, v_ref, z_ref, bcol_ref, gcol_ref, grow_ref, on_ref, o_ref,
                a_ref, b_ref, qp_ref, op_ref, s_ref, eg_ref, st_ref, *, hb, hg):
    tb = q_ref.shape[0]
    nc = tb // CHUNK
    idx_r = lax.broadcasted_iota(jnp.int32, (1, CHUNK, CHUNK), 1)
    idx_c = lax.broadcasted_iota(jnp.int32, (1, CHUNK, CHUNK), 2)
    causal = idx_r >= idx_c
    strict = idx_r > idx_c

    @pl.when(pl.program_id(2) == 0)
    def _():
        st_ref[...] = jnp.zeros(st_ref.shape, F32)

    heads = lambda ref: lambda h: ref[:, h * C_HEAD_DIM:(h + 1) * C_HEAD_DIM].astype(F32).reshape(
        nc, CHUNK, C_HEAD_DIM)
    cols = lambda ref: lambda h: ref[:, h:h + 1].reshape(nc, CHUNK, 1)
    for h0 in range(0, hb, hg):
        stack = lambda f: jnp.concatenate([f(h) for h in range(h0, h0 + hg)], axis=0)
        hsl = slice(h0, h0 + hg)
        q = stack(heads(q_ref))
        k = stack(heads(k_ref))
        v = stack(heads(v_ref))
        beta = stack(cols(bcol_ref))
        gc = stack(cols(gcol_ref))
        gr = stack(lambda h: grow_ref[h])
        decay = jnp.exp(jnp.where(causal, gc - gr, -jnp.inf))
        kb = k * beta
        kbf = k.astype(BF16)
        m = jnp.where(strict, _bdot_nt(kb.astype(BF16), kbf) * decay, 0.0)
        tinv = _unit_lower_inverse(m, idx_r, idx_c)
        rhs = jnp.concatenate([v * beta, kb * jnp.exp(gc)], axis=-1)
        sol = _bdot(tinv.astype(BF16), rhs.astype(BF16)).astype(BF16)
        ub = sol[:, :, :C_HEAD_DIM]
        wb = sol[:, :, C_HEAD_DIM:]
        glast = gc[:, CHUNK - 1:CHUNK, :]
        kd = (k * jnp.exp(glast - gc)).astype(BF16)
        aqk = (_bdot_nt(q.astype(BF16), kbf) * decay).astype(BF16)
        unstack = lambda a: a.reshape((hg, nc) + a.shape[1:])
        a_ref[hsl] = unstack((-_bdot_tn(kd, wb)).astype(BF16))
        b_ref[hsl] = unstack(_bdot_tn(kd, ub).astype(BF16))
        qp_ref[hsl] = unstack((q * jnp.exp(gc) - _bdot(aqk, wb)).astype(BF16))
        op_ref[hsl] = unstack(_bdot(aqk, ub))
        eg_ref[hsl] = unstack(jnp.broadcast_to(jnp.exp(glast), (hg * nc, 1, C_HEAD_DIM)))

    def step(c, carry):
        for h in range(hb):
            s = st_ref[h]
            sb = s.astype(BF16)
            s_ref[h, c] = sb
            st_ref[h] = s * eg_ref[h, c] + _dot(a_ref[h, c], sb) + b_ref[h, c].astype(F32)
        return carry

    lax.fori_loop(0, nc, step, 0)

    for h in range(hb):
        hs = slice(h * C_HEAD_DIM, (h + 1) * C_HEAD_DIM)
        o = (_bdot(qp_ref[h], s_ref[h]) + op_ref[h]).reshape(tb, C_HEAD_DIM)
        ms = jnp.mean(o * o, axis=-1, keepdims=True)
        on = o * lax.rsqrt(ms + NORM_EPS) * on_ref[...]
        z = z_ref[:, hs].astype(F32)
        o_ref[:, hs] = (on * _silu(z)).astype(o_ref.dtype)


def _gdn(q3, k3, v3, p3, bcol, gcol, grow, on_g, hb, tb):
    b, t, _ = q3.shape
    nc = tb // CHUNK
    wd = hb * C_HEAD_DIM
    hg = min(hb, max(1, GDN_BATCH // nc))
    kern = functools.partial(_gdn_kernel, hb=hb, hg=hg)
    qkv = pl.BlockSpec((None, tb, wd), lambda i, j, s: (i, s, j))
    return pl.pallas_call(
        kern,
        grid=(b, C_HEADS // hb, t // tb),
        in_specs=[
            qkv, qkv, qkv,
            pl.BlockSpec((None, tb, wd), lambda i, j, s: (i, s, OFF_ZC // wd + j)),
            pl.BlockSpec((None, None, tb, hb), lambda i, j, s: (i, j, s, 0)),
            pl.BlockSpec((None, None, tb, hb), lambda i, j, s: (i, j, s, 0)),
            pl.BlockSpec((None, hb, nc, 1, CHUNK), lambda i, j, s: (i, j, s, 0, 0)),
            pl.BlockSpec((1, C_HEAD_DIM), lambda i, j, s: (0, 0)),
        ],
        out_specs=pl.BlockSpec((None, tb, wd), lambda i, j, s: (i, s, j)),
        out_shape=jax.ShapeDtypeStruct((b, t, C_WIDTH), BF16),
        scratch_shapes=[
            pltpu.VMEM((hb, nc, C_HEAD_DIM, C_HEAD_DIM), BF16),
            pltpu.VMEM((hb, nc, C_HEAD_DIM, C_HEAD_DIM), BF16),
            pltpu.VMEM((hb, nc, CHUNK, C_HEAD_DIM), BF16),
            pltpu.VMEM((hb, nc, CHUNK, C_HEAD_DIM), F32),
            pltpu.VMEM((hb, nc, C_HEAD_DIM, C_HEAD_DIM), BF16),
            pltpu.VMEM((hb, nc, 1, C_HEAD_DIM), F32),
            pltpu.VMEM((hb, C_HEAD_DIM, C_HEAD_DIM), F32),
        ],
        compiler_params=_cparams(("parallel", "parallel", "arbitrary")),
        name="gdn_delta_rule",
    )(q3, k3, v3, p3, bcol, gcol, grow, on_g)


def _merge_kernel(ua_ref, ub_ref, uc_ref, g0_ref, g1_ref, g2_ref, wa_ref, wb_ref, wc_ref, o_ref):
    ya = _dot(ua_ref[...], wa_ref[...])
    yb = _dot(ub_ref[...], wb_ref[...])
    yc = _dot(uc_ref[...], wc_ref[...])
    o = (_sigmoid(g0_ref[...].astype(F32)) * ya + _sigmoid(g1_ref[...].astype(F32)) * yb
         + _sigmoid(g2_ref[...].astype(F32)) * yc)
    o_ref[...] = o.astype(o_ref.dtype)


def _merge(ua, ub, uc, p, wa, wb, wc, tm, tn):
    n = ua.shape[0]
    gate = lambda br: pl.BlockSpec((tm, tn), lambda i, j: (i, (OFF_GATE + br * D_MODEL) // tn + j))
    return pl.pallas_call(
        _merge_kernel,
        grid=(n // tm, D_MODEL // tn),
        in_specs=[
            pl.BlockSpec((tm, A_WIDTH), lambda i, j: (i, 0)),
            pl.BlockSpec((tm, B_WIDTH), lambda i, j: (i, 0)),
            pl.BlockSpec((tm, C_WIDTH), lambda i, j: (i, 0)),
            gate(0), gate(1), gate(2),
            pl.BlockSpec((A_WIDTH, tn), lambda i, j: (0, j)),
            pl.BlockSpec((B_WIDTH, tn), lambda i, j: (0, j)),
            pl.BlockSpec((C_WIDTH, tn), lambda i, j: (0, j)),
        ],
        out_specs=pl.BlockSpec((tm, tn), lambda i, j: (i, j)),
        out_shape=jax.ShapeDtypeStruct((n, D_MODEL), BF16),
        compiler_params=_cparams(("parallel", "arbitrary")),
        name="gate_merge",
    )(ua, ub, uc, p, p, p, wa, wb, wc)


def _out_kernel(x_ref, m_ref, w_ref, fg_ref, o_ref, *, final):
    y = x_ref[...] + _dot(m_ref[...], w_ref[...])
    if final:
        ms = jnp.mean(y * y, axis=-1, keepdims=True)
        y = y * lax.rsqrt(ms + NORM_EPS) * fg_ref[...]
    o_ref[...] = y


def _out_proj(x2d, merged, w, fg, tm, final):
    n, d = x2d.shape
    kern = functools.partial(_out_kernel, final=final)
    return pl.pallas_call(
        kern,
        grid=(n // tm,),
        in_specs=[
            pl.BlockSpec((tm, d), lambda i: (i, 0)),
            pl.BlockSpec((tm, d), lambda i: (i, 0)),
            pl.BlockSpec((d, d), lambda i: (0, 0)),
            pl.BlockSpec((1, d), lambda i: (0, 0)),
        ],
        out_specs=pl.BlockSpec((tm, d), lambda i: (i, 0)),
        out_shape=jax.ShapeDtypeStruct((n, d), F32),
        compiler_params=_cparams(("parallel",)),
        name="out_proj",
    )(x2d, merged, w, fg)


_SPLITS = (A_WIDTH, KV_RANK, IDX_HEADS * IDX_DIM, IDX_DIM, IDX_HEADS, A_WIDTH, B_WIDTH, B_WIDTH,
           B_WIDTH, 3 * C_WIDTH, C_HEADS, C_HEADS, C_WIDTH, N_BRANCH * D_MODEL)


def _repack_w_in(w):
    offs = np.concatenate([[0], np.cumsum(_SPLITS)])
    seg = [w[:, offs[i]:offs[i + 1]] for i in range(len(_SPLITS))]
    (q_a, c_kv, q_idx, k_idx, idx_w, z_a, glu_v, glu_g, z_b, qkv, beta, alpha, z_c, gates) = seg
    big = jnp.concatenate([q_a, q_idx, z_a, glu_v, glu_g, z_b, qkv, z_c, gates, c_kv], axis=1)
    pad = jnp.zeros((w.shape[0], SM_COLS - (SM_ALPHA + C_HEADS)), w.dtype)
    small = jnp.concatenate([k_idx, idx_w, beta, alpha, pad], axis=1)
    return big.astype(BF16), small.astype(BF16)


def _pad_lanes(vec, off):
    out = jnp.zeros((1, SM_COLS), F32)
    return out.at[0, off:off + vec.shape[0]].set(vec.astype(F32))


def _pick(total, pref):
    for c in pref:
        if total % c == 0:
            return c
    return total


def _tiles(n, t):
    return dict(tm_proj=_pick(n, (1024, 512, 256)), tq=_pick(t, (256, 128)),
                tk=_pick(t, (512, 256, 128)), tt=_pick(t, (512, 256, 128)),
                tm=_pick(n, (512, 256)), hb=8, tb=_pick(t, (512, 256, 128)))


def kernel(x, norm_g, w_in, kv_norm_g, idx_k_norm_g, w_uk, w_uv, w_proj_a, conv_w_b, conv_b_b,
           ln_g_b, ln_b_b, w_proj_b, conv_w_c, a_log, dt_bias, onorm_g_c, w_proj_c, w_out, final_g):
    b, t, _ = x.shape
    return _forward(_tiles(b * t, t), x, norm_g, w_in, kv_norm_g, idx_k_norm_g, w_uk, w_uv, w_proj_a,
                    conv_w_b, conv_b_b, ln_g_b, ln_b_b, w_proj_b, conv_w_c, a_log, dt_bias,
                    onorm_g_c, w_proj_c, w_out, final_g)


def _forward(tiles, x, norm_g, w_in, kv_norm_g, idx_k_norm_g, w_uk, w_uv, w_proj_a, conv_w_b,
             conv_b_b, ln_g_b, ln_b_b, w_proj_b, conv_w_c, a_log, dt_bias, onorm_g_c, w_proj_c,
             w_out, final_g):
    b, t, d = x.shape
    n = b * t
    depth = norm_g.shape[0]
    topk = min(TOPK_MAX, t // 4)
    nc = t // CHUNK
    tm_proj, tq, tk, tt, tm, hb, tb = (
        tiles[k] for k in ("tm_proj", "tq", "tk", "tt", "tm", "hb", "tb"))

    x2d = x.reshape(n, d)
    for l in range(depth):
        w_big, w_small = _repack_w_in(w_in[l])
        p = _rms_proj(x2d, norm_g[l][None, :], w_big, BF16, tm_proj, 512)
        sm = _rms_proj(x2d, norm_g[l][None, :], w_small, F32, tm_proj, SM_COLS)
        p3 = p.reshape(b, t, P_COLS)
        sm3 = sm.reshape(b, t, SM_COLS)

        kmat, vmat_t = _kv_expand(p3, kv_norm_g[l][None, :].astype(F32),
                                  w_uk[l].reshape(KV_RANK, A_WIDTH).astype(BF16),
                                  w_uv[l].reshape(KV_RANK, A_WIDTH).T.astype(BF16), tk)
        u_a = _dsa(p3, sm3, kmat, vmat_t, _pad_lanes(idx_k_norm_g[l], SM_KIDX), tq, tk, topk)

        cw_b = jnp.concatenate(
            [conv_w_b[l].astype(F32), jnp.zeros((CONF_HALO - CONV_K, B_WIDTH), F32)], axis=0)
        u_b = _conformer(p3, cw_b, conv_b_b[l][None, :].astype(F32), ln_g_b[l][None, :].astype(F32),
                         ln_b_b[l][None, :].astype(F32), tt)

        cw_c = jnp.concatenate(
            [conv_w_c[l].astype(F32), jnp.zeros((SUBLANES - SHORT_K, 3 * C_WIDTH), F32)], axis=0)
        q3, k3, v3 = _gdn_pre(p3, cw_c, tt)
        gate = _gdn_gate(sm3, _pad_lanes(a_log[l], SM_ALPHA), _pad_lanes(dt_bias[l], SM_ALPHA))
        beta = jnp.transpose(gate[:, :, SM_BETA:SM_BETA + C_HEADS], (0, 2, 1))
        gcum = jnp.transpose(gate[:, :, SM_ALPHA:SM_ALPHA + C_HEADS], (0, 2, 1))
        col = lambda a: jnp.transpose(a.reshape(b, C_HEADS // hb, hb, t), (0, 1, 3, 2))
        u_c = _gdn(q3, k3, v3, p3, col(beta), col(gcum),
                   gcum.reshape(b, C_HEADS, nc, 1, CHUNK), onorm_g_c[l][None, :].astype(F32), hb, tb)

        merged = _merge(u_a.reshape(n, A_WIDTH), u_b.reshape(n, B_WIDTH), u_c.reshape(n, C_WIDTH), p,
                        w_proj_a[l].astype(BF16), w_proj_b[l].astype(BF16), w_proj_c[l].astype(BF16),
                        tm, 512)
        x2d = _out_proj(x2d, merged, w_out[l].astype(BF16), final_g[None, :].astype(F32), tm,
                        l == depth - 1)
    return x2d.reshape(b, t, d)
```

```python
import functools
import math

import jax
import jax.numpy as jnp
import numpy as np
from jax import lax
from jax.experimental import pallas as pl
from jax.experimental.pallas import tpu as pltpu

F32 = jnp.float32
BF16 = jnp.bfloat16

D_MODEL = 2048
NORM_EPS = 1e-6
LN_EPS = 1e-5
L2_EPS = 1e-6
A_HEADS = 8
A_HEAD_DIM = 128
A_WIDTH = A_HEADS * A_HEAD_DIM
KV_RANK = 512
IDX_HEADS = 16
IDX_DIM = 64
TOPK_MAX = 256
B_WIDTH = 1024
CONV_K = 31
C_HEADS = 16
C_HEAD_DIM = 128
C_WIDTH = C_HEADS * C_HEAD_DIM
SHORT_K = 4
CHUNK = 64
N_BRANCH = 3

LANES = 128
SUBLANES = 8
VMEM_LIMIT = 56 * 1024 * 1024

OFF_QA = 0
OFF_QIDX = OFF_QA + A_WIDTH
OFF_ZA = OFF_QIDX + IDX_HEADS * IDX_DIM
OFF_GLUV = OFF_ZA + A_WIDTH
OFF_GLUG = OFF_GLUV + B_WIDTH
OFF_ZB = OFF_GLUG + B_WIDTH
OFF_QKV = OFF_ZB + B_WIDTH
OFF_ZC = OFF_QKV + 3 * C_WIDTH
OFF_GATE = OFF_ZC + C_WIDTH
OFF_CKV = OFF_GATE + N_BRANCH * D_MODEL
P_COLS = OFF_CKV + KV_RANK
SM_KIDX = 0
SM_IDXW = SM_KIDX + IDX_DIM
SM_BETA = SM_IDXW + IDX_HEADS
SM_ALPHA = SM_BETA + C_HEADS
SM_COLS = LANES

INT_MIN = -(2**31)


def _cparams(sem):
    return pltpu.CompilerParams(dimension_semantics=sem, vmem_limit_bytes=VMEM_LIMIT)


def _sigmoid(x):
    return 1.0 / (1.0 + jnp.exp(-x))


def _silu(x):
    return x * _sigmoid(x)


def _dot(a, b):
    return jnp.dot(a, b, preferred_element_type=F32)


def _dot_nt(a, b):
    return lax.dot_general(a, b, (((1,), (1,)), ((), ())), preferred_element_type=F32)


def _dot_tn(a, b):
    return lax.dot_general(a, b, (((0,), (0,)), ((), ())), preferred_element_type=F32)


def _bdot(a, b):
    return lax.dot_general(a, b, (((2,), (1,)), ((0,), (0,))), preferred_element_type=F32)


def _bdot_tn(a, b):
    return lax.dot_general(a, b, (((1,), (1,)), ((0,), (0,))), preferred_element_type=F32)


def _bdot_nt(a, b):
    return lax.dot_general(a, b, (((2,), (2,)), ((0,), (0,))), preferred_element_type=F32)


def _rmsnorm_rows(x, g):
    ms = jnp.mean(x * x, axis=-1, keepdims=True)
    return x * lax.rsqrt(ms + NORM_EPS) * g


def _rms_kernel(x_ref, g_ref, o_ref):
    o_ref[...] = _rmsnorm_rows(x_ref[...], g_ref[...]).astype(o_ref.dtype)


def _rmsnorm(x2d, g, tm):
    n, d = x2d.shape
    return pl.pallas_call(
        _rms_kernel,
        grid=(n // tm,),
        in_specs=[pl.BlockSpec((tm, d), lambda i: (i, 0)), pl.BlockSpec((1, d), lambda i: (0, 0))],
        out_specs=pl.BlockSpec((tm, d), lambda i: (i, 0)),
        out_shape=jax.ShapeDtypeStruct((n, d), BF16),
        compiler_params=_cparams(("parallel",)),
        name="rmsnorm",
    )(x2d, g)


def _proj_kernel(xn_ref, w_ref, ws_ref, o_ref, os_ref):
    xn = xn_ref[...]
    o_ref[...] = _dot(xn, w_ref[...]).astype(o_ref.dtype)

    @pl.when(pl.program_id(1) == 0)
    def _():
        os_ref[...] = _dot(xn, ws_ref[...])


def _proj(xn, w, w_small, tm, tn):
    n, d = xn.shape
    cols = w.shape[1]
    return pl.pallas_call(
        _proj_kernel,
        grid=(n // tm, cols // tn),
        in_specs=[
            pl.BlockSpec((tm, d), lambda i, j: (i, 0)),
            pl.BlockSpec((d, tn), lambda i, j: (0, j)),
            pl.BlockSpec((d, SM_COLS), lambda i, j: (0, 0)),
        ],
        out_specs=[pl.BlockSpec((tm, tn), lambda i, j: (i, j)),
                   pl.BlockSpec((tm, SM_COLS), lambda i, j: (i, 0))],
        out_shape=[jax.ShapeDtypeStruct((n, cols), BF16), jax.ShapeDtypeStruct((n, SM_COLS), F32)],
        compiler_params=_cparams(("parallel", "arbitrary")),
        name="in_proj",
    )(xn, w, w_small)


POS_SPLIT = 64
POS_TERMS = 3


def _kv_kernel(c_ref, g_ref, wk_ref, wvt_ref, k_ref, vt_ref):
    tk = c_ref.shape[0]
    c = c_ref[...].astype(F32)
    ms = jnp.mean(c * c, axis=-1, keepdims=True)
    cn = (c * lax.rsqrt(ms + NORM_EPS) * g_ref[...]).astype(BF16)
    k = _dot(cn, wk_ref[...]).astype(BF16)
    vt_ref[...] = _dot_nt(wvt_ref[...], cn).astype(BF16)
    pos = pl.program_id(1) * tk + lax.broadcasted_iota(jnp.int32, (tk, A_HEAD_DIM), 0)
    lane = lax.broadcasted_iota(jnp.int32, (tk, A_HEAD_DIM), 1)
    feat = jnp.where(lane < POS_TERMS, pos // POS_SPLIT,
                     jnp.where(lane < 2 * POS_TERMS, pos % POS_SPLIT, 0)).astype(F32).astype(BF16)
    for h in range(A_HEADS):
        k_ref[:, 2 * h * A_HEAD_DIM:(2 * h + 1) * A_HEAD_DIM] = k[:, h * A_HEAD_DIM:(h + 1) * A_HEAD_DIM]
        k_ref[:, (2 * h + 1) * A_HEAD_DIM:(2 * h + 2) * A_HEAD_DIM] = feat


def _kv_expand(p3, g, wk, wvt, tk):
    b, t, _ = p3.shape
    return pl.pallas_call(
        _kv_kernel,
        grid=(b, t // tk),
        in_specs=[
            pl.BlockSpec((None, tk, KV_RANK), lambda i, j: (i, j, OFF_CKV // KV_RANK)),
            pl.BlockSpec((1, KV_RANK), lambda i, j: (0, 0)),
            pl.BlockSpec((KV_RANK, A_WIDTH), lambda i, j: (0, 0)),
            pl.BlockSpec((A_WIDTH, KV_RANK), lambda i, j: (0, 0)),
        ],
        out_specs=[
            pl.BlockSpec((None, tk, 2 * A_WIDTH), lambda i, j: (i, j, 0)),
            pl.BlockSpec((None, None, A_WIDTH, tk), lambda i, j: (i, j, 0, 0)),
        ],
        out_shape=[jax.ShapeDtypeStruct((b, t, 2 * A_WIDTH), BF16),
                   jax.ShapeDtypeStruct((b, t // tk, A_WIDTH, tk), BF16)],
        compiler_params=_cparams(("parallel", "parallel")),
        name="kv_expand",
    )(p3, g, wk, wvt)


LOG2E = 1.4426950408889634
MASK_BIAS = -1e30
M_INIT = -1e29
CNT_ROWS = 32
SEL_UNROLL = 4


def _dsa_kernel(qa_ref, qi_ref, za_ref, smq_ref, smk_ref, gk_ref, k_ref, vt_ref, o_ref,
                kn_ref, sc_ref, qs_ref, m_ref, l_ref, acc_ref, s_ref, *, tq, tk, topk):
    qb = pl.program_id(1)

    @pl.when(qb == 0)
    def _():
        kx = smk_ref[...]
        lane = lax.broadcasted_iota(jnp.int32, (1, SM_COLS), 1)
        is_k = lane < IDX_DIM
        kx = jnp.where(is_k, kx, 0.0)
        ms = jnp.sum(kx * kx, axis=-1, keepdims=True) * (1.0 / IDX_DIM)
        kn = kx * lax.rsqrt(ms + NORM_EPS) * gk_ref[...]
        kn_ref[0] = kn.astype(BF16)
        kn_ref[1] = pltpu.roll(kn, IDX_DIM, 1).astype(BF16)

    nkc = (qb * tq + tq + tk - 1) // tk
    qpos = qb * tq + lax.broadcasted_iota(jnp.int32, (1, tq), 1)
    kpos0 = lax.broadcasted_iota(jnp.int32, (tk, 1), 0)
    w_t = smq_ref[...].T * (IDX_HEADS ** -0.5 * IDX_DIM ** -0.5)

    def score_chunk(c, carry):
        k0 = pl.multiple_of(c * tk, tk)
        acc = jnp.zeros((tk, tq), F32)
        for hp in range(IDX_HEADS // 2):
            qpair = qi_ref[:, hp * LANES:(hp + 1) * LANES]
            for half in range(2):
                row = SM_IDXW + 2 * hp + half
                kc = kn_ref[half, pl.ds(k0, tk), :]
                acc = acc + w_t[row:row + 1, :] * jnp.maximum(_dot_nt(kc, qpair), 0.0)
        bits = pltpu.bitcast(acc, jnp.int32)
        skey = jnp.where(bits < 0, bits ^ jnp.int32(0x7FFFFFFF), bits)
        sc_ref[c] = jnp.where(k0 + kpos0 <= qpos, skey, jnp.int32(INT_MIN))
        return carry

    lax.fori_loop(0, nkc, score_chunk, 0)

    def count_ge(cand):
        def body(c, acc):
            ind = jnp.where(sc_ref[c] >= cand, 1.0, 0.0)
            return acc + jnp.sum(ind.reshape(tk // CNT_ROWS, CNT_ROWS, tq), axis=0)

        acc = lax.fori_loop(0, nkc, body, jnp.zeros((CNT_ROWS, tq), F32))
        return jnp.sum(acc, axis=0, keepdims=True)

    fkeep = jnp.float32(topk)

    def sel_cond(st):
        i, _, done = st
        return (i < 32) & (jnp.min(done) < 0.5)

    def sel_step(st):
        i, prefix, done = st
        for u in range(SEL_UNROLL):
            cand = prefix ^ (jnp.int32(1) << (31 - u - i))
            cnt = count_ge(cand)
            prefix = jnp.where((done < 0.5) & (cnt >= fkeep), cand, prefix)
            done = jnp.where(cnt == fkeep, 1.0, done)
        return i + SEL_UNROLL, prefix, done

    done0 = jnp.where(qpos < topk, 1.0, 0.0)
    _, kth, _ = lax.while_loop(
        sel_cond, sel_step, (jnp.int32(0), jnp.full((1, tq), INT_MIN, jnp.int32), done0))
    thr = jnp.maximum(kth, jnp.int32(INT_MIN + 1))

    lane = lax.broadcasted_iota(jnp.int32, (1, A_HEAD_DIM), 1)
    for h in range(A_HEADS):
        slope = 2.0 ** (-8.0 * (h + 1) / A_HEADS) * LOG2E
        row = jnp.zeros((1, A_HEAD_DIM), F32)
        rest = np.float32(slope)
        for i in range(POS_TERMS):
            piece = np.float32(rest.astype(BF16))
            rest = np.float32(rest - piece)
            row = jnp.where(lane == i, float(piece) * POS_SPLIT, row)
            row = jnp.where(lane == POS_TERMS + i, float(piece), row)
        hs = slice(h * A_HEAD_DIM, (h + 1) * A_HEAD_DIM)
        qs_ref[:, 2 * h * A_HEAD_DIM:(2 * h + 1) * A_HEAD_DIM] = (
            qa_ref[:, hs].astype(F32) * (A_HEAD_DIM ** -0.5 * LOG2E)).astype(BF16)
        qs_ref[:, (2 * h + 1) * A_HEAD_DIM:(2 * h + 2) * A_HEAD_DIM] = jnp.broadcast_to(
            row, (tq, A_HEAD_DIM)).astype(BF16)
    m_ref[...] = jnp.full(m_ref.shape, M_INIT, F32)
    l_ref[...] = jnp.zeros(l_ref.shape, F32)
    acc_ref[...] = jnp.zeros(acc_ref.shape, F32)

    def qk_scores(c, slot):
        k0 = pl.multiple_of(c * tk, tk)
        for h in range(A_HEADS):
            hs2 = slice(2 * h * A_HEAD_DIM, (2 * h + 2) * A_HEAD_DIM)
            s_ref[slot, h] = _dot_nt(k_ref[pl.ds(k0, tk), hs2], qs_ref[:, hs2])

    def att_chunk(c, carry):
        slot = c % 2
        bias = jnp.where(sc_ref[c] >= thr, 0.0, MASK_BIAS)
        for h in range(A_HEADS):
            hs = slice(h * A_HEAD_DIM, (h + 1) * A_HEAD_DIM)
            s = s_ref[slot, h] + bias
            m_old = m_ref[h]
            m_new = jnp.maximum(m_old, jnp.max(s, axis=0, keepdims=True))
            p = jnp.exp2(s - m_new)
            alpha = jnp.exp2(m_old - m_new)
            l_ref[h] = alpha * l_ref[h] + jnp.sum(p, axis=0, keepdims=True)
            acc_ref[hs, :] = alpha * acc_ref[hs, :] + _dot(vt_ref[c, hs, :], p.astype(BF16))
            m_ref[h] = m_new
        qk_scores(jnp.minimum(c + 1, nkc - 1), 1 - slot)
        return carry

    qk_scores(0, 0)
    lax.fori_loop(0, nkc, att_chunk, 0)
    for h in range(A_HEADS):
        hs = slice(h * A_HEAD_DIM, (h + 1) * A_HEAD_DIM)
        z = za_ref[:, hs].astype(F32)
        o_ref[:, hs] = ((acc_ref[hs, :] / l_ref[h]).T * _silu(z)).astype(o_ref.dtype)


def _dsa(p3, sm3, kmat, vmat_t, gk, tq, tk, topk):
    b, t, _ = p3.shape
    kern = functools.partial(_dsa_kernel, tq=tq, tk=tk, topk=topk)
    return pl.pallas_call(
        kern,
        grid=(b, t // tq),
        in_specs=[
            pl.BlockSpec((None, tq, A_WIDTH), lambda i, j: (i, j, OFF_QA // A_WIDTH)),
            pl.BlockSpec((None, tq, A_WIDTH), lambda i, j: (i, j, OFF_QIDX // A_WIDTH)),
            pl.BlockSpec((None, tq, A_WIDTH), lambda i, j: (i, j, OFF_ZA // A_WIDTH)),
            pl.BlockSpec((None, tq, SM_COLS), lambda i, j: (i, j, 0)),
            pl.BlockSpec((None, t, SM_COLS), lambda i, j: (i, 0, 0)),
            pl.BlockSpec((1, SM_COLS), lambda i, j: (0, 0)),
            pl.BlockSpec((None, t, 2 * A_WIDTH), lambda i, j: (i, 0, 0)),
            pl.BlockSpec((None, t // tk, A_WIDTH, tk), lambda i, j: (i, 0, 0, 0)),
        ],
        out_specs=pl.BlockSpec((None, tq, A_WIDTH), lambda i, j: (i, j, 0)),
        out_shape=jax.ShapeDtypeStruct((b, t, A_WIDTH), BF16),
        scratch_shapes=[
            pltpu.VMEM((2, t, LANES), BF16),
            pltpu.VMEM((t // tk, tk, tq), jnp.int32),
            pltpu.VMEM((tq, 2 * A_WIDTH), BF16),
            pltpu.VMEM((A_HEADS, 1, tq), F32),
            pltpu.VMEM((A_HEADS, 1, tq), F32),
            pltpu.VMEM((A_WIDTH, tq), F32),
            pltpu.VMEM((2, A_HEADS, tk, tq), F32),
        ],
        compiler_params=_cparams(("parallel", "arbitrary")),
        name="dsa_attention",
    )(p3, p3, p3, sm3, sm3, gk, kmat, vmat_t)


CONF_HALO = 32
CONF_ROWS = 64
CONF_WIN = CONF_ROWS + CONF_HALO


def _conf_kernel(v_ref, g_ref, vh_ref, gh_ref, z_ref, cw_ref, cb_ref, lg_ref, lb_ref, o_ref,
                 hp_ref, y_ref, *, tt):
    i = pl.program_id(1)
    hh = vh_ref[...].astype(F32) * _sigmoid(gh_ref[...].astype(F32))
    hp_ref[0:CONF_HALO, :] = jnp.where(i > 0, hh, 0.0)
    hp_ref[CONF_HALO:CONF_HALO + tt, :] = v_ref[...].astype(F32) * _sigmoid(g_ref[...].astype(F32))

    first = CONF_HALO - (CONV_K - 1)
    for ct in range(B_WIDTH // LANES):
        cs = slice(ct * LANES, (ct + 1) * LANES)

        def rows(r, carry, cs=cs):
            base = pl.multiple_of(r * CONF_ROWS, CONF_ROWS)
            win = hp_ref[pl.ds(base, CONF_WIN), cs]
            acc = jnp.zeros((CONF_ROWS, LANES), F32)
            for rem in range(SUBLANES):
                rot = win if rem == 0 else pltpu.roll(win, CONF_WIN - rem, 0)
                for j in range(CONV_K):
                    if (first + j) % SUBLANES == rem:
                        a = (first + j) - rem
                        acc = acc + cw_ref[j:j + 1, cs] * rot[a:a + CONF_ROWS]
            y_ref[pl.ds(base, CONF_ROWS), cs] = acc + cb_ref[:, cs]
            return carry

        lax.fori_loop(0, tt // CONF_ROWS, rows, 0)

    y = y_ref[...]
    mu = jnp.mean(y, axis=-1, keepdims=True)
    yc = y - mu
    var = jnp.mean(yc * yc, axis=-1, keepdims=True)
    yn = yc * lax.rsqrt(var + LN_EPS) * lg_ref[...] + lb_ref[...]
    z = z_ref[...].astype(F32)
    o_ref[...] = (_silu(yn) * _silu(z)).astype(o_ref.dtype)


def _conformer(p3, cw, cb, lg, lb, tt):
    b, t, _ = p3.shape
    hb = tt // CONF_HALO
    kern = functools.partial(_conf_kernel, tt=tt)
    cur = lambda off: pl.BlockSpec((None, tt, B_WIDTH), lambda i, j: (i, j, off // B_WIDTH))
    halo = lambda off: pl.BlockSpec(
        (None, CONF_HALO, B_WIDTH), lambda i, j: (i, jnp.maximum(j * hb - 1, 0), off // B_WIDTH))
    vec = lambda r: pl.BlockSpec((r, B_WIDTH), lambda i, j: (0, 0))
    return pl.pallas_call(
        kern,
        grid=(b, t // tt),
        in_specs=[cur(OFF_GLUV), cur(OFF_GLUG), halo(OFF_GLUV), halo(OFF_GLUG), cur(OFF_ZB),
                  vec(CONF_HALO), vec(1), vec(1), vec(1)],
        out_specs=pl.BlockSpec((None, tt, B_WIDTH), lambda i, j: (i, j, 0)),
        out_shape=jax.ShapeDtypeStruct((b, t, B_WIDTH), BF16),
        scratch_shapes=[pltpu.VMEM((CONF_HALO + tt, B_WIDTH), F32), pltpu.VMEM((tt, B_WIDTH), F32)],
        compiler_params=_cparams(("parallel", "parallel")),
        name="conformer",
    )(p3, p3, p3, p3, p3, cw, cb, lg, lb)


GDN_HALO = 8


def _gdn_pre_kernel(x_ref, xh_ref, cw_ref, q_ref, k_ref, v_ref, *, tt):
    i = pl.program_id(1)
    first = GDN_HALO - (SHORT_K - 1)
    outs = (q_ref, k_ref, v_ref)
    for ct in range(3 * C_HEADS):
        cs = slice(ct * LANES, (ct + 1) * LANES)
        which, head = divmod(ct, C_HEADS)
        halo = jnp.where(i > 0, xh_ref[:, cs].astype(F32), 0.0)
        xp = jnp.concatenate([halo, x_ref[:, cs].astype(F32)], axis=0)
        acc = cw_ref[0:1, cs] * xp[first:first + tt]
        for j in range(1, SHORT_K):
            acc = acc + cw_ref[j:j + 1, cs] * xp[first + j:first + j + tt]
        y = _silu(acc)
        if which < 2:
            y = y * lax.rsqrt(jnp.sum(y * y, axis=-1, keepdims=True) + L2_EPS)
        if which == 0:
            y = y * (C_HEAD_DIM ** -0.5)
        outs[which][:, head * LANES:(head + 1) * LANES] = y.astype(BF16)


def _gdn_pre(p3, cw, tt):
    b, t, _ = p3.shape
    hb = tt // GDN_HALO
    w3 = 3 * C_WIDTH
    kern = functools.partial(_gdn_pre_kernel, tt=tt)
    out = pl.BlockSpec((None, tt, C_WIDTH), lambda i, j: (i, j, 0))
    return pl.pallas_call(
        kern,
        grid=(b, t // tt),
        in_specs=[
            pl.BlockSpec((None, tt, w3), lambda i, j: (i, j, OFF_QKV // w3)),
            pl.BlockSpec((None, GDN_HALO, w3), lambda i, j: (i, jnp.maximum(j * hb - 1, 0), OFF_QKV // w3)),
            pl.BlockSpec((SUBLANES, w3), lambda i, j: (0, 0)),
        ],
        out_specs=[out, out, out],
        out_shape=[jax.ShapeDtypeStruct((b, t, C_WIDTH), BF16)] * 3,
        compiler_params=_cparams(("parallel", "parallel")),
        name="gdn_pre",
    )(p3, p3, cw)


def _gdn_gate_kernel(sm_ref, a_ref, dt_ref, o_ref):
    t = sm_ref.shape[0]
    x = sm_ref[...]
    lane = lax.broadcasted_iota(jnp.int32, (1, SM_COLS), 1)
    beta = _sigmoid(x)
    y = x + dt_ref[...]
    softplus = jnp.maximum(y, 0.0) + jnp.log1p(jnp.exp(-jnp.abs(y)))
    g = -jnp.exp(a_ref[...]) * softplus
    row = lax.broadcasted_iota(jnp.int32, (CHUNK, CHUNK), 0)
    col = lax.broadcasted_iota(jnp.int32, (CHUNK, CHUNK), 1)
    tril = (row >= col).astype(F32)
    is_beta = (lane >= SM_BETA) & (lane < SM_BETA + C_HEADS)
    for c in range(t // CHUNK):
        rs = slice(c * CHUNK, (c + 1) * CHUNK)
        gc = jnp.dot(tril, g[rs], preferred_element_type=F32, precision=lax.Precision.HIGHEST)
        o_ref[rs, :] = jnp.where(is_beta, beta[rs], gc)


def _gdn_gate(sm3, a_vec, dt_vec):
    b, t, _ = sm3.shape
    return pl.pallas_call(
        _gdn_gate_kernel,
        grid=(b,),
        in_specs=[
            pl.BlockSpec((None, t, SM_COLS), lambda i: (i, 0, 0)),
            pl.BlockSpec((1, SM_COLS), lambda i: (0, 0)),
            pl.BlockSpec((1, SM_COLS), lambda i: (0, 0)),
        ],
        out_specs=pl.BlockSpec((None, t, SM_COLS), lambda i: (i, 0, 0)),
        out_shape=jax.ShapeDtypeStruct((b, t, SM_COLS), F32),
        compiler_params=_cparams(("parallel",)),
        name="gdn_gate",
    )(sm3, a_vec, dt_vec)


GDN_BATCH = 32


def _unit_lower_inverse(m, idx_r, idx_c):
    n = m.shape[-1]
    eye = (idx_r == idx_c).astype(F32)
    blk = lambda s: (idx_r // s) == (idx_c // s)
    m0 = jnp.where(blk(8), m, 0.0)
    m0b = m0.astype(BF16)
    a2 = _bdot(m0b, m0b)
    p = eye - m0
    p = p + _bdot(p.astype(BF16), a2.astype(BF16))
    a4 = _bdot(a2.astype(BF16), a2.astype(BF16))
    p = p + _bdot(p.astype(BF16), a4.astype(BF16))
    s = 8
    while s < n:
        lo = jnp.where(blk(2 * s) & jnp.logical_not(blk(s)), m, 0.0)
        pb = p.astype(BF16)
        p = p - _bdot(pb, _bdot(lo.astype(BF16), pb).astype(BF16))
        s *= 2
    return p


def _gdn_kernel(q_ref, k_ref, v_ref, z_ref, bcol_ref, gcol_ref, grow_ref, on_ref, o_ref,
                a_ref, b_ref, qp_ref, op_ref, s_ref, eg_ref, st_ref, *, hb, hg):
    tb = q_ref.shape[0]
    nc = tb // CHUNK
    idx_r = lax.broadcasted_iota(jnp.int32, (1, CHUNK, CHUNK), 1)
    idx_c = lax.broadcasted_iota(jnp.int32, (1, CHUNK, CHUNK), 2)
    causal = idx_r >= idx_c
    strict = idx_r > idx_c

    @pl.when(pl.program_id(2) == 0)
    def _():
        st_ref[...] = jnp.zeros(st_ref.shape, F32)

    heads = lambda ref: lambda h: ref[:, h * C_HEAD_DIM:(h + 1) * C_HEAD_DIM].astype(F32).reshape(
        nc, CHUNK, C_HEAD_DIM)
    cols = lambda ref: lambda h: ref[:, h:h + 1].reshape(nc, CHUNK, 1)
    for h0 in range(0, hb, hg):
        stack = lambda f: jnp.concatenate([f(h) for h in range(h0, h0 + hg)], axis=0)
        hsl = slice(h0, h0 + hg)
        q = stack(heads(q_ref))
        k = stack(heads(k_ref))
        v = stack(heads(v_ref))
        beta = stack(cols(bcol_ref))
        gc = stack(cols(gcol_ref))
        gr = stack(lambda h: grow_ref[h])
        decay = jnp.exp(jnp.where(causal, gc - gr, -jnp.inf))
        kb = k * beta
        kbf = k.astype(BF16)
        m = jnp.where(strict, _bdot_nt(kb.astype(BF16), kbf) * decay, 0.0)
        tinv = _unit_lower_inverse(m, idx_r, idx_c)
        rhs = jnp.concatenate([v * beta, kb * jnp.exp(gc)], axis=-1)
        sol = _bdot(tinv.astype(BF16), rhs.astype(BF16)).astype(BF16)
        ub = sol[:, :, :C_HEAD_DIM]
        wb = sol[:, :, C_HEAD_DIM:]
        glast = gc[:, CHUNK - 1:CHUNK, :]
        kd = (k * jnp.exp(glast - gc)).astype(BF16)
        aqk = (_bdot_nt(q.astype(BF16), kbf) * decay).astype(BF16)
        unstack = lambda a: a.reshape((hg, nc) + a.shape[1:])
        a_ref[hsl] = unstack((-_bdot_tn(kd, wb)).astype(BF16))
        b_ref[hsl] = unstack(_bdot_tn(kd, ub).astype(BF16))
        qp_ref[hsl] = unstack((q * jnp.exp(gc) - _bdot(aqk, wb)).astype(BF16))
        op_ref[hsl] = unstack(_bdot(aqk, ub))
        eg_ref[hsl] = unstack(jnp.broadcast_to(jnp.exp(glast), (hg * nc, 1, C_HEAD_DIM)))

    def step(c, carry):
        for h in range(hb):
            s = st_ref[h]
            sb = s.astype(BF16)
            s_ref[h, c] = sb
            st_ref[h] = s * eg_ref[h, c] + _dot(a_ref[h, c], sb) + b_ref[h, c].astype(F32)
        return carry

    lax.fori_loop(0, nc, step, 0)

    for h in range(hb):
        hs = slice(h * C_HEAD_DIM, (h + 1) * C_HEAD_DIM)
        o = (_bdot(qp_ref[h], s_ref[h]) + op_ref[h]).reshape(tb, C_HEAD_DIM)
        ms = jnp.mean(o * o, axis=-1, keepdims=True)
        on = o * lax.rsqrt(ms + NORM_EPS) * on_ref[...]
        z = z_ref[:, hs].astype(F32)
        o_ref[:, hs] = (on * _silu(z)).astype(o_ref.dtype)


def _gdn(q3, k3, v3, p3, bcol, gcol, grow, on_g, hb, tb):
    b, t, _ = q3.shape
    nc = tb // CHUNK
    wd = hb * C_HEAD_DIM
    hg = min(hb, max(1, GDN_BATCH // nc))
    kern = functools.partial(_gdn_kernel, hb=hb, hg=hg)
    qkv = pl.BlockSpec((None, tb, wd), lambda i, j, s: (i, s, j))
    return pl.pallas_call(
        kern,
        grid=(b, C_HEADS // hb, t // tb),
        in_specs=[
            qkv, qkv, qkv,
            pl.BlockSpec((None, tb, wd), lambda i, j, s: (i, s, OFF_ZC // wd + j)),
            pl.BlockSpec((None, None, tb, hb), lambda i, j, s: (i, j, s, 0)),
            pl.BlockSpec((None, None, tb, hb), lambda i, j, s: (i, j, s, 0)),
            pl.BlockSpec((None, hb, nc, 1, CHUNK), lambda i, j, s: (i, j, s, 0, 0)),
            pl.BlockSpec((1, C_HEAD_DIM), lambda i, j, s: (0, 0)),
        ],
        out_specs=pl.BlockSpec((None, tb, wd), lambda i, j, s: (i, s, j)),
        out_shape=jax.ShapeDtypeStruct((b, t, C_WIDTH), BF16),
        scratch_shapes=[
            pltpu.VMEM((hb, nc, C_HEAD_DIM, C_HEAD_DIM), BF16),
            pltpu.VMEM((hb, nc, C_HEAD_DIM, C_HEAD_DIM), BF16),
            pltpu.VMEM((hb, nc, CHUNK, C_HEAD_DIM), BF16),
            pltpu.VMEM((hb, nc, CHUNK, C_HEAD_DIM), F32),
            pltpu.VMEM((hb, nc, C_HEAD_DIM, C_HEAD_DIM), BF16),
            pltpu.VMEM((hb, nc, 1, C_HEAD_DIM), F32),
            pltpu.VMEM((hb, C_HEAD_DIM, C_HEAD_DIM), F32),
        ],
        compiler_params=_cparams(("parallel", "parallel", "arbitrary")),
        name="gdn_delta_rule",
    )(q3, k3, v3, p3, bcol, gcol, grow, on_g)


def _merge_kernel(ua_ref, ub_ref, uc_ref, g0_ref, g1_ref, g2_ref, wa_ref, wb_ref, wc_ref, o_ref):
    ya = _dot(ua_ref[...], wa_ref[...])
    yb = _dot(ub_ref[...], wb_ref[...])
    yc = _dot(uc_ref[...], wc_ref[...])
    o = (_sigmoid(g0_ref[...].astype(F32)) * ya + _sigmoid(g1_ref[...].astype(F32)) * yb
         + _sigmoid(g2_ref[...].astype(F32)) * yc)
    o_ref[...] = o.astype(o_ref.dtype)


def _merge(ua, ub, uc, p, wa, wb, wc, tm, tn):
    n = ua.shape[0]
    gate = lambda br: pl.BlockSpec((tm, tn), lambda i, j: (i, (OFF_GATE + br * D_MODEL) // tn + j))
    return pl.pallas_call(
        _merge_kernel,
        grid=(n // tm, D_MODEL // tn),
        in_specs=[
            pl.BlockSpec((tm, A_WIDTH), lambda i, j: (i, 0)),
            pl.BlockSpec((tm, B_WIDTH), lambda i, j: (i, 0)),
            pl.BlockSpec((tm, C_WIDTH), lambda i, j: (i, 0)),
            gate(0), gate(1), gate(2),
            pl.BlockSpec((A_WIDTH, tn), lambda i, j: (0, j)),
            pl.BlockSpec((B_WIDTH, tn), lambda i, j: (0, j)),
            pl.BlockSpec((C_WIDTH, tn), lambda i, j: (0, j)),
        ],
        out_specs=pl.BlockSpec((tm, tn), lambda i, j: (i, j)),
        out_shape=jax.ShapeDtypeStruct((n, D_MODEL), BF16),
        compiler_params=_cparams(("parallel", "arbitrary")),
        name="gate_merge",
    )(ua, ub, uc, p, p, p, wa, wb, wc)


def _out_kernel(x_ref, m_ref, w_ref, g_ref, *o_refs, final):
    y = x_ref[...] + _dot(m_ref[...], w_ref[...])
    if final:
        o_refs[0][...] = _rmsnorm_rows(y, g_ref[...])
    else:
        o_refs[0][...] = y
        o_refs[1][...] = _rmsnorm_rows(y, g_ref[...]).astype(BF16)


def _out_proj(x2d, merged, w, g, tm, final):
    n, d = x2d.shape
    kern = functools.partial(_out_kernel, final=final)
    row = pl.BlockSpec((tm, d), lambda i: (i, 0))
    return pl.pallas_call(
        kern,
        grid=(n // tm,),
        in_specs=[row, row, pl.BlockSpec((d, d), lambda i: (0, 0)), pl.BlockSpec((1, d), lambda i: (0, 0))],
        out_specs=row if final else [row, row],
        out_shape=(jax.ShapeDtypeStruct((n, d), F32) if final else
                   [jax.ShapeDtypeStruct((n, d), F32), jax.ShapeDtypeStruct((n, d), BF16)]),
        compiler_params=_cparams(("parallel",)),
        name="out_proj",
    )(x2d, merged, w, g)


_SPLITS = (A_WIDTH, KV_RANK, IDX_HEADS * IDX_DIM, IDX_DIM, IDX_HEADS, A_WIDTH, B_WIDTH, B_WIDTH,
           B_WIDTH, 3 * C_WIDTH, C_HEADS, C_HEADS, C_WIDTH, N_BRANCH * D_MODEL)


def _repack_w_in(w):
    offs = np.concatenate([[0], np.cumsum(_SPLITS)])
    seg = [w[:, offs[i]:offs[i + 1]] for i in range(len(_SPLITS))]
    (q_a, c_kv, q_idx, k_idx, idx_w, z_a, glu_v, glu_g, z_b, qkv, beta, alpha, z_c, gates) = seg
    big = jnp.concatenate([q_a, q_idx, z_a, glu_v, glu_g, z_b, qkv, z_c, gates, c_kv], axis=1)
    pad = jnp.zeros((w.shape[0], SM_COLS - (SM_ALPHA + C_HEADS)), w.dtype)
    small = jnp.concatenate([k_idx, idx_w, beta, alpha, pad], axis=1)
    return big.astype(BF16), small.astype(BF16)


def _pad_lanes(vec, off):
    out = jnp.zeros((1, SM_COLS), F32)
    return out.at[0, off:off + vec.shape[0]].set(vec.astype(F32))


def _pick(total, pref):
    for c in pref:
        if total % c == 0:
            return c
    return total


def _tiles(n, t):
    return dict(tm_proj=_pick(n, (2048, 1024, 512, 256)), tq=_pick(t, (256, 128)),
                tk=_pick(t, (256, 128)), tt=_pick(t, (512, 256, 128)),
                tm=_pick(n, (512, 256)), tm_merge=_pick(n, (1024, 512, 256)), hb=8,
                tb=_pick(t, (512, 256, 128)))


def kernel(x, norm_g, w_in, kv_norm_g, idx_k_norm_g, w_uk, w_uv, w_proj_a, conv_w_b, conv_b_b,
           ln_g_b, ln_b_b, w_proj_b, conv_w_c, a_log, dt_bias, onorm_g_c, w_proj_c, w_out, final_g):
    b, t, _ = x.shape
    return _forward(_tiles(b * t, t), x, norm_g, w_in, kv_norm_g, idx_k_norm_g, w_uk, w_uv, w_proj_a,
                    conv_w_b, conv_b_b, ln_g_b, ln_b_b, w_proj_b, conv_w_c, a_log, dt_bias,
                    onorm_g_c, w_proj_c, w_out, final_g)


def _forward(tiles, x, norm_g, w_in, kv_norm_g, idx_k_norm_g, w_uk, w_uv, w_proj_a, conv_w_b,
             conv_b_b, ln_g_b, ln_b_b, w_proj_b, conv_w_c, a_log, dt_bias, onorm_g_c, w_proj_c,
             w_out, final_g):
    b, t, d = x.shape
    n = b * t
    depth = norm_g.shape[0]
    topk = min(TOPK_MAX, t // 4)
    nc = t // CHUNK
    tm_proj, tq, tk, tt, tm, tm_merge, hb, tb = (
        tiles[k] for k in ("tm_proj", "tq", "tk", "tt", "tm", "tm_merge", "hb", "tb"))

    x2d = x.reshape(n, d)
    xn = _rmsnorm(x2d, norm_g[0][None, :].astype(F32), tm)
    for l in range(depth):
        w_big, w_small = _repack_w_in(w_in[l])
        p, sm = _proj(xn, w_big, w_small, tm_proj, 512)
        p3 = p.reshape(b, t, P_COLS)
        sm3 = sm.reshape(b, t, SM_COLS)

        kmat, vmat_t = _kv_expand(p3, kv_norm_g[l][None, :].astype(F32),
                                  w_uk[l].reshape(KV_RANK, A_WIDTH).astype(BF16),
                                  w_uv[l].reshape(KV_RANK, A_WIDTH).T.astype(BF16), tk)
        u_a = _dsa(p3, sm3, kmat, vmat_t, _pad_lanes(idx_k_norm_g[l], SM_KIDX), tq, tk, topk)

        cw_b = jnp.concatenate(
            [conv_w_b[l].astype(F32), jnp.zeros((CONF_HALO - CONV_K, B_WIDTH), F32)], axis=0)
        u_b = _conformer(p3, cw_b, conv_b_b[l][None, :].astype(F32), ln_g_b[l][None, :].astype(F32),
                         ln_b_b[l][None, :].astype(F32), tt)

        cw_c = jnp.concatenate(
            [conv_w_c[l].astype(F32), jnp.zeros((SUBLANES - SHORT_K, 3 * C_WIDTH), F32)], axis=0)
        q3, k3, v3 = _gdn_pre(p3, cw_c, tt)
        gate = _gdn_gate(sm3, _pad_lanes(a_log[l], SM_ALPHA), _pad_lanes(dt_bias[l], SM_ALPHA))
        beta = jnp.transpose(gate[:, :, SM_BETA:SM_BETA + C_HEADS], (0, 2, 1))
        gcum = jnp.transpose(gate[:, :, SM_ALPHA:SM_ALPHA + C_HEADS], (0, 2, 1))
        col = lambda a: jnp.transpose(a.reshape(b, C_HEADS // hb, hb, t), (0, 1, 3, 2))
        u_c = _gdn(q3, k3, v3, p3, col(beta), col(gcum),
                   gcum.reshape(b, C_HEADS, nc, 1, CHUNK), onorm_g_c[l][None, :].astype(F32), hb, tb)

        merged = _merge(u_a.reshape(n, A_WIDTH), u_b.reshape(n, B_WIDTH), u_c.reshape(n, C_WIDTH), p,
                        w_proj_a[l].astype(BF16), w_proj_b[l].astype(BF16), w_proj_c[l].astype(BF16),
                        tm_merge, 512)
        if l == depth - 1:
            x2d = _out_proj(x2d, merged, w_out[l].astype(BF16), final_g[None, :].astype(F32), tm, True)
        else:
            x2d, xn = _out_proj(x2d, merged, w_out[l].astype(BF16),
                                norm_g[l + 1][None, :].astype(F32), tm, False)
    return x2d.reshape(b, t, d)
```

```python
import functools
import math

import jax
import jax.numpy as jnp
import numpy as np
from jax import lax
from jax.experimental import pallas as pl
from jax.experimental.pallas import tpu as pltpu

F32 = jnp.float32
BF16 = jnp.bfloat16

D_MODEL = 2048
NORM_EPS = 1e-6
LN_EPS = 1e-5
L2_EPS = 1e-6
A_HEADS = 8
A_HEAD_DIM = 128
A_WIDTH = A_HEADS * A_HEAD_DIM
KV_RANK = 512
IDX_HEADS = 16
IDX_DIM = 64
TOPK_MAX = 256
B_WIDTH = 1024
CONV_K = 31
C_HEADS = 16
C_HEAD_DIM = 128
C_WIDTH = C_HEADS * C_HEAD_DIM
SHORT_K = 4
CHUNK = 64
N_BRANCH = 3

LANES = 128
SUBLANES = 8
VMEM_LIMIT = 56 * 1024 * 1024

OFF_QA = 0
OFF_QIDX = OFF_QA + A_WIDTH
OFF_ZA = OFF_QIDX + IDX_HEADS * IDX_DIM
OFF_GLUV = OFF_ZA + A_WIDTH
OFF_GLUG = OFF_GLUV + B_WIDTH
OFF_ZB = OFF_GLUG + B_WIDTH
OFF_ZC = OFF_ZB + B_WIDTH
OFF_GATE = OFF_ZC + C_WIDTH
OFF_CKV = OFF_GATE + N_BRANCH * D_MODEL
P_COLS = OFF_CKV + KV_RANK
SM_KIDX = 0
SM_IDXW = SM_KIDX + IDX_DIM
SM_BETA = SM_IDXW + IDX_HEADS
SM_ALPHA = SM_BETA + C_HEADS
SM_COLS = LANES

INT_MIN = -(2**31)


def _cparams(sem):
    return pltpu.CompilerParams(dimension_semantics=sem, vmem_limit_bytes=VMEM_LIMIT)


def _sigmoid(x):
    return 1.0 / (1.0 + jnp.exp(-x))


def _silu(x):
    return x * _sigmoid(x)


def _dot(a, b):
    return jnp.dot(a, b, preferred_element_type=F32)


def _dot_nt(a, b):
    return lax.dot_general(a, b, (((1,), (1,)), ((), ())), preferred_element_type=F32)


def _dot_tn(a, b):
    return lax.dot_general(a, b, (((0,), (0,)), ((), ())), preferred_element_type=F32)


def _bdot(a, b):
    return lax.dot_general(a, b, (((2,), (1,)), ((0,), (0,))), preferred_element_type=F32)


def _bdot_tn(a, b):
    return lax.dot_general(a, b, (((1,), (1,)), ((0,), (0,))), preferred_element_type=F32)


def _bdot_nt(a, b):
    return lax.dot_general(a, b, (((2,), (2,)), ((0,), (0,))), preferred_element_type=F32)


def _rmsnorm_rows(x, g):
    ms = jnp.mean(x * x, axis=-1, keepdims=True)
    return x * lax.rsqrt(ms + NORM_EPS) * g


def _rms_kernel(x_ref, g_ref, o_ref):
    o_ref[...] = _rmsnorm_rows(x_ref[...], g_ref[...]).astype(o_ref.dtype)


def _rmsnorm(x2d, g, tm):
    n, d = x2d.shape
    return pl.pallas_call(
        _rms_kernel,
        grid=(n // tm,),
        in_specs=[pl.BlockSpec((tm, d), lambda i: (i, 0)), pl.BlockSpec((1, d), lambda i: (0, 0))],
        out_specs=pl.BlockSpec((tm, d), lambda i: (i, 0)),
        out_shape=jax.ShapeDtypeStruct((n, d), BF16),
        compiler_params=_cparams(("parallel",)),
        name="rmsnorm",
    )(x2d, g)


def _proj_kernel(xn_ref, w_ref, ws_ref, o_ref, os_ref):
    xn = xn_ref[...]
    o_ref[...] = _dot(xn, w_ref[...]).astype(o_ref.dtype)

    @pl.when(pl.program_id(1) == 0)
    def _():
        os_ref[...] = _dot(xn, ws_ref[...])


def _proj(xn, w, w_small, tm, tn):
    n, d = xn.shape
    cols = w.shape[1]
    return pl.pallas_call(
        _proj_kernel,
        grid=(n // tm, cols // tn),
        in_specs=[
            pl.BlockSpec((tm, d), lambda i, j: (i, 0)),
            pl.BlockSpec((d, tn), lambda i, j: (0, j)),
            pl.BlockSpec((d, SM_COLS), lambda i, j: (0, 0)),
        ],
        out_specs=[pl.BlockSpec((tm, tn), lambda i, j: (i, j)),
                   pl.BlockSpec((tm, SM_COLS), lambda i, j: (i, 0))],
        out_shape=[jax.ShapeDtypeStruct((n, cols), BF16), jax.ShapeDtypeStruct((n, SM_COLS), F32)],
        compiler_params=_cparams(("parallel", "arbitrary")),
        name="in_proj",
    )(xn, w, w_small)


POS_SPLIT = 64
POS_TERMS = 3


def _kv_kernel(c_ref, g_ref, wk_ref, wvt_ref, k_ref, vt_ref):
    tk = c_ref.shape[0]
    c = c_ref[...].astype(F32)
    ms = jnp.mean(c * c, axis=-1, keepdims=True)
    cn = (c * lax.rsqrt(ms + NORM_EPS) * g_ref[...]).astype(BF16)
    k = _dot(cn, wk_ref[...]).astype(BF16)
    vt_ref[...] = _dot_nt(wvt_ref[...], cn).astype(BF16)
    pos = pl.program_id(1) * tk + lax.broadcasted_iota(jnp.int32, (tk, A_HEAD_DIM), 0)
    lane = lax.broadcasted_iota(jnp.int32, (tk, A_HEAD_DIM), 1)
    feat = jnp.where(lane < POS_TERMS, pos // POS_SPLIT,
                     jnp.where(lane < 2 * POS_TERMS, pos % POS_SPLIT, 0)).astype(F32).astype(BF16)
    for h in range(A_HEADS):
        k_ref[:, 2 * h * A_HEAD_DIM:(2 * h + 1) * A_HEAD_DIM] = k[:, h * A_HEAD_DIM:(h + 1) * A_HEAD_DIM]
        k_ref[:, (2 * h + 1) * A_HEAD_DIM:(2 * h + 2) * A_HEAD_DIM] = feat


def _kv_expand(p3, g, wk, wvt, tk):
    b, t, _ = p3.shape
    return pl.pallas_call(
        _kv_kernel,
        grid=(b, t // tk),
        in_specs=[
            pl.BlockSpec((None, tk, KV_RANK), lambda i, j: (i, j, OFF_CKV // KV_RANK)),
            pl.BlockSpec((1, KV_RANK), lambda i, j: (0, 0)),
            pl.BlockSpec((KV_RANK, A_WIDTH), lambda i, j: (0, 0)),
            pl.BlockSpec((A_WIDTH, KV_RANK), lambda i, j: (0, 0)),
        ],
        out_specs=[
            pl.BlockSpec((None, tk, 2 * A_WIDTH), lambda i, j: (i, j, 0)),
            pl.BlockSpec((None, None, A_WIDTH, tk), lambda i, j: (i, j, 0, 0)),
        ],
        out_shape=[jax.ShapeDtypeStruct((b, t, 2 * A_WIDTH), BF16),
                   jax.ShapeDtypeStruct((b, t // tk, A_WIDTH, tk), BF16)],
        compiler_params=_cparams(("parallel", "parallel")),
        name="kv_expand",
    )(p3, g, wk, wvt)


LOG2E = 1.4426950408889634
MASK_BIAS = -1e30
M_INIT = -1e29
CNT_ROWS = 32
SEL_UNROLL = 4


def _dsa_kernel(qa_ref, qi_ref, za_ref, smq_ref, smk_ref, gk_ref, k_ref, vt_ref, o_ref,
                kn_ref, sc_ref, qs_ref, m_ref, l_ref, acc_ref, s_ref, *, tq, tk, topk):
    qb = pl.program_id(1)

    @pl.when(qb == 0)
    def _():
        kx = smk_ref[...]
        lane = lax.broadcasted_iota(jnp.int32, (1, SM_COLS), 1)
        is_k = lane < IDX_DIM
        kx = jnp.where(is_k, kx, 0.0)
        ms = jnp.sum(kx * kx, axis=-1, keepdims=True) * (1.0 / IDX_DIM)
        kn = kx * lax.rsqrt(ms + NORM_EPS) * gk_ref[...]
        kn_ref[0] = kn.astype(BF16)
        kn_ref[1] = pltpu.roll(kn, IDX_DIM, 1).astype(BF16)

    nkc = (qb * tq + tq + tk - 1) // tk
    qpos = qb * tq + lax.broadcasted_iota(jnp.int32, (1, tq), 1)
    kpos0 = lax.broadcasted_iota(jnp.int32, (tk, 1), 0)
    w_t = smq_ref[...].T * (IDX_HEADS ** -0.5 * IDX_DIM ** -0.5)

    def score_chunk(c, carry):
        k0 = pl.multiple_of(c * tk, tk)
        acc = jnp.zeros((tk, tq), F32)
        for hp in range(IDX_HEADS // 2):
            qpair = qi_ref[:, hp * LANES:(hp + 1) * LANES]
            for half in range(2):
                row = SM_IDXW + 2 * hp + half
                kc = kn_ref[half, pl.ds(k0, tk), :]
                acc = acc + w_t[row:row + 1, :] * jnp.maximum(_dot_nt(kc, qpair), 0.0)
        bits = pltpu.bitcast(acc, jnp.int32)
        skey = jnp.where(bits < 0, bits ^ jnp.int32(0x7FFFFFFF), bits)
        sc_ref[c] = jnp.where(k0 + kpos0 <= qpos, skey, jnp.int32(INT_MIN))
        return carry

    lax.fori_loop(0, nkc, score_chunk, 0)

    def count_ge(cand):
        def body(c, acc):
            ind = jnp.where(sc_ref[c] >= cand, 1.0, 0.0)
            return acc + jnp.sum(ind.reshape(tk // CNT_ROWS, CNT_ROWS, tq), axis=0)

        acc = lax.fori_loop(0, nkc, body, jnp.zeros((CNT_ROWS, tq), F32))
        return jnp.sum(acc, axis=0, keepdims=True)

    fkeep = jnp.float32(topk)

    def sel_cond(st):
        i, _, done = st
        return (i < 32) & (jnp.min(done) < 0.5)

    def sel_step(st):
        i, prefix, done = st
        for u in range(SEL_UNROLL):
            cand = prefix ^ (jnp.int32(1) << (31 - u - i))
            cnt = count_ge(cand)
            prefix = jnp.where((done < 0.5) & (cnt >= fkeep), cand, prefix)
            done = jnp.where(cnt == fkeep, 1.0, done)
        return i + SEL_UNROLL, prefix, done

    done0 = jnp.where(qpos < topk, 1.0, 0.0)
    _, kth, _ = lax.while_loop(
        sel_cond, sel_step, (jnp.int32(0), jnp.full((1, tq), INT_MIN, jnp.int32), done0))
    thr = jnp.maximum(kth, jnp.int32(INT_MIN + 1))

    lane = lax.broadcasted_iota(jnp.int32, (1, A_HEAD_DIM), 1)
    for h in range(A_HEADS):
        slope = 2.0 ** (-8.0 * (h + 1) / A_HEADS) * LOG2E
        row = jnp.zeros((1, A_HEAD_DIM), F32)
        rest = np.float32(slope)
        for i in range(POS_TERMS):
            piece = np.float32(rest.astype(BF16))
            rest = np.float32(rest - piece)
            row = jnp.where(lane == i, float(piece) * POS_SPLIT, row)
            row = jnp.where(lane == POS_TERMS + i, float(piece), row)
        hs = slice(h * A_HEAD_DIM, (h + 1) * A_HEAD_DIM)
        qs_ref[:, 2 * h * A_HEAD_DIM:(2 * h + 1) * A_HEAD_DIM] = (
            qa_ref[:, hs].astype(F32) * (A_HEAD_DIM ** -0.5 * LOG2E)).astype(BF16)
        qs_ref[:, (2 * h + 1) * A_HEAD_DIM:(2 * h + 2) * A_HEAD_DIM] = jnp.broadcast_to(
            row, (tq, A_HEAD_DIM)).astype(BF16)
    m_ref[...] = jnp.full(m_ref.shape, M_INIT, F32)
    l_ref[...] = jnp.zeros(l_ref.shape, F32)
    acc_ref[...] = jnp.zeros(acc_ref.shape, F32)

    def qk_scores(c, slot):
        k0 = pl.multiple_of(c * tk, tk)
        for h in range(A_HEADS):
            hs2 = slice(2 * h * A_HEAD_DIM, (2 * h + 2) * A_HEAD_DIM)
            s_ref[slot, h] = _dot_nt(k_ref[pl.ds(k0, tk), hs2], qs_ref[:, hs2])

    def att_chunk(c, carry):
        slot = c % 2
        bias = jnp.where(sc_ref[c] >= thr, 0.0, MASK_BIAS)
        for h in range(A_HEADS):
            hs = slice(h * A_HEAD_DIM, (h + 1) * A_HEAD_DIM)
            s = s_ref[slot, h] + bias
            m_old = m_ref[h]
            m_new = jnp.maximum(m_old, jnp.max(s, axis=0, keepdims=True))
            p = jnp.exp2(s - m_new)
            alpha = jnp.exp2(m_old - m_new)
            l_ref[h] = alpha * l_ref[h] + jnp.sum(p, axis=0, keepdims=True)
            acc_ref[hs, :] = alpha * acc_ref[hs, :] + _dot(vt_ref[c, hs, :], p.astype(BF16))
            m_ref[h] = m_new
        qk_scores(jnp.minimum(c + 1, nkc - 1), 1 - slot)
        return carry

    qk_scores(0, 0)
    lax.fori_loop(0, nkc, att_chunk, 0)
    for h in range(A_HEADS):
        hs = slice(h * A_HEAD_DIM, (h + 1) * A_HEAD_DIM)
        z = za_ref[:, hs].astype(F32)
        o_ref[:, hs] = ((acc_ref[hs, :] / l_ref[h]).T * _silu(z)).astype(o_ref.dtype)


def _dsa(p3, sm3, kmat, vmat_t, gk, tq, tk, topk):
    b, t, _ = p3.shape
    kern = functools.partial(_dsa_kernel, tq=tq, tk=tk, topk=topk)
    return pl.pallas_call(
        kern,
        grid=(b, t // tq),
        in_specs=[
            pl.BlockSpec((None, tq, A_WIDTH), lambda i, j: (i, j, OFF_QA // A_WIDTH)),
            pl.BlockSpec((None, tq, A_WIDTH), lambda i, j: (i, j, OFF_QIDX // A_WIDTH)),
            pl.BlockSpec((None, tq, A_WIDTH), lambda i, j: (i, j, OFF_ZA // A_WIDTH)),
            pl.BlockSpec((None, tq, SM_COLS), lambda i, j: (i, j, 0)),
            pl.BlockSpec((None, t, SM_COLS), lambda i, j: (i, 0, 0)),
            pl.BlockSpec((1, SM_COLS), lambda i, j: (0, 0)),
            pl.BlockSpec((None, t, 2 * A_WIDTH), lambda i, j: (i, 0, 0)),
            pl.BlockSpec((None, t // tk, A_WIDTH, tk), lambda i, j: (i, 0, 0, 0)),
        ],
        out_specs=pl.BlockSpec((None, tq, A_WIDTH), lambda i, j: (i, j, 0)),
        out_shape=jax.ShapeDtypeStruct((b, t, A_WIDTH), BF16),
        scratch_shapes=[
            pltpu.VMEM((2, t, LANES), BF16),
            pltpu.VMEM((t // tk, tk, tq), jnp.int32),
            pltpu.VMEM((tq, 2 * A_WIDTH), BF16),
            pltpu.VMEM((A_HEADS, 1, tq), F32),
            pltpu.VMEM((A_HEADS, 1, tq), F32),
            pltpu.VMEM((A_WIDTH, tq), F32),
            pltpu.VMEM((2, A_HEADS, tk, tq), F32),
        ],
        compiler_params=_cparams(("parallel", "arbitrary")),
        name="dsa_attention",
    )(p3, p3, p3, sm3, sm3, gk, kmat, vmat_t)


CONF_HALO = 32
CONF_ROWS = 64
CONF_WIN = CONF_ROWS + CONF_HALO


def _conf_kernel(v_ref, g_ref, vh_ref, gh_ref, z_ref, cw_ref, cb_ref, lg_ref, lb_ref, o_ref,
                 hp_ref, y_ref, *, tt):
    i = pl.program_id(1)
    hh = vh_ref[...].astype(F32) * _sigmoid(gh_ref[...].astype(F32))
    hp_ref[0:CONF_HALO, :] = jnp.where(i > 0, hh, 0.0)
    hp_ref[CONF_HALO:CONF_HALO + tt, :] = v_ref[...].astype(F32) * _sigmoid(g_ref[...].astype(F32))

    first = CONF_HALO - (CONV_K - 1)
    for ct in range(B_WIDTH // LANES):
        cs = slice(ct * LANES, (ct + 1) * LANES)

        def rows(r, carry, cs=cs):
            base = pl.multiple_of(r * CONF_ROWS, CONF_ROWS)
            win = hp_ref[pl.ds(base, CONF_WIN), cs]
            acc = jnp.zeros((CONF_ROWS, LANES), F32)
            for rem in range(SUBLANES):
                rot = win if rem == 0 else pltpu.roll(win, CONF_WIN - rem, 0)
                for j in range(CONV_K):
                    if (first + j) % SUBLANES == rem:
                        a = (first + j) - rem
                        acc = acc + cw_ref[j:j + 1, cs] * rot[a:a + CONF_ROWS]
            y_ref[pl.ds(base, CONF_ROWS), cs] = acc + cb_ref[:, cs]
            return carry

        lax.fori_loop(0, tt // CONF_ROWS, rows, 0)

    y = y_ref[...]
    mu = jnp.mean(y, axis=-1, keepdims=True)
    yc = y - mu
    var = jnp.mean(yc * yc, axis=-1, keepdims=True)
    yn = yc * lax.rsqrt(var + LN_EPS) * lg_ref[...] + lb_ref[...]
    z = z_ref[...].astype(F32)
    o_ref[...] = (_silu(yn) * _silu(z)).astype(o_ref.dtype)


def _conformer(p3, cw, cb, lg, lb, tt):
    b, t, _ = p3.shape
    hb = tt // CONF_HALO
    kern = functools.partial(_conf_kernel, tt=tt)
    cur = lambda off: pl.BlockSpec((None, tt, B_WIDTH), lambda i, j: (i, j, off // B_WIDTH))
    halo = lambda off: pl.BlockSpec(
        (None, CONF_HALO, B_WIDTH), lambda i, j: (i, jnp.maximum(j * hb - 1, 0), off // B_WIDTH))
    vec = lambda r: pl.BlockSpec((r, B_WIDTH), lambda i, j: (0, 0))
    return pl.pallas_call(
        kern,
        grid=(b, t // tt),
        in_specs=[cur(OFF_GLUV), cur(OFF_GLUG), halo(OFF_GLUV), halo(OFF_GLUG), cur(OFF_ZB),
                  vec(CONF_HALO), vec(1), vec(1), vec(1)],
        out_specs=pl.BlockSpec((None, tt, B_WIDTH), lambda i, j: (i, j, 0)),
        out_shape=jax.ShapeDtypeStruct((b, t, B_WIDTH), BF16),
        scratch_shapes=[pltpu.VMEM((CONF_HALO + tt, B_WIDTH), F32), pltpu.VMEM((tt, B_WIDTH), F32)],
        compiler_params=_cparams(("parallel", "parallel")),
        name="conformer",
    )(p3, p3, p3, p3, p3, cw, cb, lg, lb)


GDN_HALO = 8
QKV_ROWS = 128


def _qkv_kernel(xn_ref, w_ref, cw_ref, o_ref, *, seq):
    j = pl.program_id(1)
    tm, tn = o_ref.shape
    first = GDN_HALO - (SHORT_K - 1)
    tiles_per_part = C_WIDTH // tn
    is_v = j >= 2 * tiles_per_part
    qscale = jnp.where(j < tiles_per_part, C_HEAD_DIM ** -0.5, 1.0)
    rb = min(QKV_ROWS, seq)
    for s0 in range(0, tm, seq):
        tail = jnp.zeros((GDN_HALO, tn), F32)
        for r0 in range(s0, s0 + seq, rb):
            y = _dot(xn_ref[r0:r0 + rb, :], w_ref[...])
            xp = jnp.concatenate([tail, y], axis=0)
            tail = y[rb - GDN_HALO:, :]
            for ct in range(tn // LANES):
                cs = slice(ct * LANES, (ct + 1) * LANES)
                win = xp[:, cs]
                acc = cw_ref[SHORT_K - 1:SHORT_K, cs] * win[GDN_HALO:]
                for back in range(1, SHORT_K):
                    tap = SHORT_K - 1 - back
                    acc = acc + cw_ref[tap:tap + 1, cs] * pltpu.roll(win, back, 0)[GDN_HALO:]
                a = _silu(acc)
                rs = lax.rsqrt(jnp.sum(a * a, axis=-1, keepdims=True) + L2_EPS)
                o_ref[r0:r0 + rb, cs] = (a * (jnp.where(is_v, 1.0, rs) * qscale)).astype(BF16)


def _qkv_proj(xn, w, cw, tm, tn, seq):
    n, d = xn.shape
    cols = w.shape[1]
    kern = functools.partial(_qkv_kernel, seq=seq)
    return pl.pallas_call(
        kern,
        grid=(n // tm, cols // tn),
        in_specs=[
            pl.BlockSpec((tm, d), lambda i, j: (i, 0)),
            pl.BlockSpec((d, tn), lambda i, j: (0, j)),
            pl.BlockSpec((SUBLANES, tn), lambda i, j: (0, j)),
        ],
        out_specs=pl.BlockSpec((tm, tn), lambda i, j: (i, j)),
        out_shape=jax.ShapeDtypeStruct((n, cols), BF16),
        compiler_params=_cparams(("parallel", "arbitrary")),
        name="qkv_proj",
    )(xn, w, cw)


def _gdn_gate_kernel(sm_ref, a_ref, dt_ref, o_ref):
    t = sm_ref.shape[0]
    x = sm_ref[...]
    lane = lax.broadcasted_iota(jnp.int32, (1, SM_COLS), 1)
    beta = _sigmoid(x)
    y = x + dt_ref[...]
    softplus = jnp.maximum(y, 0.0) + jnp.log1p(jnp.exp(-jnp.abs(y)))
    g = -jnp.exp(a_ref[...]) * softplus
    row = lax.broadcasted_iota(jnp.int32, (CHUNK, CHUNK), 0)
    col = lax.broadcasted_iota(jnp.int32, (CHUNK, CHUNK), 1)
    tril = (row >= col).astype(F32)
    is_beta = (lane >= SM_BETA) & (lane < SM_BETA + C_HEADS)
    for c in range(t // CHUNK):
        rs = slice(c * CHUNK, (c + 1) * CHUNK)
        gc = jnp.dot(tril, g[rs], preferred_element_type=F32, precision=lax.Precision.HIGHEST)
        o_ref[rs, :] = jnp.where(is_beta, beta[rs], gc)


def _gdn_gate(sm3, a_vec, dt_vec):
    b, t, _ = sm3.shape
    return pl.pallas_call(
        _gdn_gate_kernel,
        grid=(b,),
        in_specs=[
            pl.BlockSpec((None, t, SM_COLS), lambda i: (i, 0, 0)),
            pl.BlockSpec((1, SM_COLS), lambda i: (0, 0)),
            pl.BlockSpec((1, SM_COLS), lambda i: (0, 0)),
        ],
        out_specs=pl.BlockSpec((None, t, SM_COLS), lambda i: (i, 0, 0)),
        out_shape=jax.ShapeDtypeStruct((b, t, SM_COLS), F32),
        compiler_params=_cparams(("parallel",)),
        name="gdn_gate",
    )(sm3, a_vec, dt_vec)


GDN_BATCH = 32


def _unit_lower_inverse(m, idx_r, idx_c):
    n = m.shape[-1]
    eye = (idx_r == idx_c).astype(F32)
    blk = lambda s: (idx_r // s) == (idx_c // s)
    m0 = jnp.where(blk(8), m, 0.0)
    m0b = m0.astype(BF16)
    a2 = _bdot(m0b, m0b)
    p = eye - m0
    p = p + _bdot(p.astype(BF16), a2.astype(BF16))
    a4 = _bdot(a2.astype(BF16), a2.astype(BF16))
    p = p + _bdot(p.astype(BF16), a4.astype(BF16))
    s = 8
    while s < n:
        lo = jnp.where(blk(2 * s) & jnp.logical_not(blk(s)), m, 0.0)
        pb = p.astype(BF16)
        p = p - _bdot(pb, _bdot(lo.astype(BF16), pb).astype(BF16))
        s *= 2
    return p


def _gdn_kernel(q_ref, k_ref, v_ref, z_ref, bcol_ref, gcol_ref, grow_ref, on_ref, o_ref,
                a_ref, b_ref, qp_ref, op_ref, s_ref, eg_ref, st_ref, *, hb, hg):
    tb = q_ref.shape[0]
    nc = tb // CHUNK
    idx_r = lax.broadcasted_iota(jnp.int32, (1, CHUNK, CHUNK), 1)
    idx_c = lax.broadcasted_iota(jnp.int32, (1, CHUNK, CHUNK), 2)
    causal = idx_r >= idx_c
    strict = idx_r > idx_c

    @pl.when(pl.program_id(2) == 0)
    def _():
        st_ref[...] = jnp.zeros(st_ref.shape, F32)

    heads = lambda ref: lambda h: ref[:, h * C_HEAD_DIM:(h + 1) * C_HEAD_DIM].astype(F32).reshape(
        nc, CHUNK, C_HEAD_DIM)
    cols = lambda ref: lambda h: ref[:, h:h + 1].reshape(nc, CHUNK, 1)
    for h0 in range(0, hb, hg):
        stack = lambda f: jnp.concatenate([f(h) for h in range(h0, h0 + hg)], axis=0)
        hsl = slice(h0, h0 + hg)
        q = stack(heads(q_ref))
        k = stack(heads(k_ref))
        v = stack(heads(v_ref))
        beta = stack(cols(bcol_ref))
        gc = stack(cols(gcol_ref))
        gr = stack(lambda h: grow_ref[h])
        decay = jnp.exp(jnp.where(causal, gc - gr, -jnp.inf))
        kb = k * beta
        kbf = k.astype(BF16)
        m = jnp.where(strict, _bdot_nt(kb.astype(BF16), kbf) * decay, 0.0)
        tinv = _unit_lower_inverse(m, idx_r, idx_c)
        rhs = jnp.concatenate([v * beta, kb * jnp.exp(gc)], axis=-1)
        sol = _bdot(tinv.astype(BF16), rhs.astype(BF16)).astype(BF16)
        ub = sol[:, :, :C_HEAD_DIM]
        wb = sol[:, :, C_HEAD_DIM:]
        glast = gc[:, CHUNK - 1:CHUNK, :]
        kd = (k * jnp.exp(glast - gc)).astype(BF16)
        aqk = (_bdot_nt(q.astype(BF16), kbf) * decay).astype(BF16)
        unstack = lambda a: a.reshape((hg, nc) + a.shape[1:])
        a_ref[hsl] = unstack((-_bdot_tn(kd, wb)).astype(BF16))
        b_ref[hsl] = unstack(_bdot_tn(kd, ub).astype(BF16))
        qp_ref[hsl] = unstack((q * jnp.exp(gc) - _bdot(aqk, wb)).astype(BF16))
        op_ref[hsl] = unstack(_bdot(aqk, ub))
        eg_ref[hsl] = unstack(jnp.broadcast_to(jnp.exp(glast), (hg * nc, 1, C_HEAD_DIM)))

    def step(c, carry):
        for h in range(hb):
            s = st_ref[h]
            sb = s.astype(BF16)
            s_ref[h, c] = sb
            st_ref[h] = s * eg_ref[h, c] + _dot(a_ref[h, c], sb) + b_ref[h, c].astype(F32)
        return carry

    lax.fori_loop(0, nc, step, 0)

    for h in range(hb):
        hs = slice(h * C_HEAD_DIM, (h + 1) * C_HEAD_DIM)
        o = (_bdot(qp_ref[h], s_ref[h]) + op_ref[h]).reshape(tb, C_HEAD_DIM)
        ms = jnp.mean(o * o, axis=-1, keepdims=True)
        on = o * lax.rsqrt(ms + NORM_EPS) * on_ref[...]
        z = z_ref[:, hs].astype(F32)
        o_ref[:, hs] = (on * _silu(z)).astype(o_ref.dtype)


def _gdn(qkv3, p3, bcol, gcol, grow, on_g, hb, tb):
    b, t, _ = qkv3.shape
    nc = tb // CHUNK
    wd = hb * C_HEAD_DIM
    hg = min(hb, max(1, GDN_BATCH // nc))
    kern = functools.partial(_gdn_kernel, hb=hb, hg=hg)
    part = lambda which: pl.BlockSpec(
        (None, tb, wd), lambda i, j, s: (i, s, which * (C_WIDTH // wd) + j))
    return pl.pallas_call(
        kern,
        grid=(b, C_HEADS // hb, t // tb),
        in_specs=[
            part(0), part(1), part(2),
            pl.BlockSpec((None, tb, wd), lambda i, j, s: (i, s, OFF_ZC // wd + j)),
            pl.BlockSpec((None, None, tb, hb), lambda i, j, s: (i, j, s, 0)),
            pl.BlockSpec((None, None, tb, hb), lambda i, j, s: (i, j, s, 0)),
            pl.BlockSpec((None, hb, nc, 1, CHUNK), lambda i, j, s: (i, j, s, 0, 0)),
            pl.BlockSpec((1, C_HEAD_DIM), lambda i, j, s: (0, 0)),
        ],
        out_specs=pl.BlockSpec((None, tb, wd), lambda i, j, s: (i, s, j)),
        out_shape=jax.ShapeDtypeStruct((b, t, C_WIDTH), BF16),
        scratch_shapes=[
            pltpu.VMEM((hb, nc, C_HEAD_DIM, C_HEAD_DIM), BF16),
            pltpu.VMEM((hb, nc, C_HEAD_DIM, C_HEAD_DIM), BF16),
            pltpu.VMEM((hb, nc, CHUNK, C_HEAD_DIM), BF16),
            pltpu.VMEM((hb, nc, CHUNK, C_HEAD_DIM), F32),
            pltpu.VMEM((hb, nc, C_HEAD_DIM, C_HEAD_DIM), BF16),
            pltpu.VMEM((hb, nc, 1, C_HEAD_DIM), F32),
            pltpu.VMEM((hb, C_HEAD_DIM, C_HEAD_DIM), F32),
        ],
        compiler_params=_cparams(("parallel", "parallel", "arbitrary")),
        name="gdn_delta_rule",
    )(qkv3, qkv3, qkv3, p3, bcol, gcol, grow, on_g)


def _merge_kernel(ua_ref, ub_ref, uc_ref, g0_ref, g1_ref, g2_ref, wa_ref, wb_ref, wc_ref, o_ref):
    ya = _dot(ua_ref[...], wa_ref[...])
    yb = _dot(ub_ref[...], wb_ref[...])
    yc = _dot(uc_ref[...], wc_ref[...])
    o = (_sigmoid(g0_ref[...].astype(F32)) * ya + _sigmoid(g1_ref[...].astype(F32)) * yb
         + _sigmoid(g2_ref[...].astype(F32)) * yc)
    o_ref[...] = o.astype(o_ref.dtype)


def _merge(ua, ub, uc, p, wa, wb, wc, tm, tn):
    n = ua.shape[0]
    gate = lambda br: pl.BlockSpec((tm, tn), lambda i, j: (i, (OFF_GATE + br * D_MODEL) // tn + j))
    return pl.pallas_call(
        _merge_kernel,
        grid=(n // tm, D_MODEL // tn),
        in_specs=[
            pl.BlockSpec((tm, A_WIDTH), lambda i, j: (i, 0)),
            pl.BlockSpec((tm, B_WIDTH), lambda i, j: (i, 0)),
            pl.BlockSpec((tm, C_WIDTH), lambda i, j: (i, 0)),
            gate(0), gate(1), gate(2),
            pl.BlockSpec((A_WIDTH, tn), lambda i, j: (0, j)),
            pl.BlockSpec((B_WIDTH, tn), lambda i, j: (0, j)),
            pl.BlockSpec((C_WIDTH, tn), lambda i, j: (0, j)),
        ],
        out_specs=pl.BlockSpec((tm, tn), lambda i, j: (i, j)),
        out_shape=jax.ShapeDtypeStruct((n, D_MODEL), BF16),
        compiler_params=_cparams(("parallel", "arbitrary")),
        name="gate_merge",
    )(ua, ub, uc, p, p, p, wa, wb, wc)


def _out_kernel(x_ref, m_ref, w_ref, g_ref, *o_refs, final):
    y = x_ref[...] + _dot(m_ref[...], w_ref[...])
    if final:
        o_refs[0][...] = _rmsnorm_rows(y, g_ref[...])
    else:
        o_refs[0][...] = y
        o_refs[1][...] = _rmsnorm_rows(y, g_ref[...]).astype(BF16)


def _out_proj(x2d, merged, w, g, tm, final):
    n, d = x2d.shape
    kern = functools.partial(_out_kernel, final=final)
    row = pl.BlockSpec((tm, d), lambda i: (i, 0))
    return pl.pallas_call(
        kern,
        grid=(n // tm,),
        in_specs=[row, row, pl.BlockSpec((d, d), lambda i: (0, 0)), pl.BlockSpec((1, d), lambda i: (0, 0))],
        out_specs=row if final else [row, row],
        out_shape=(jax.ShapeDtypeStruct((n, d), F32) if final else
                   [jax.ShapeDtypeStruct((n, d), F32), jax.ShapeDtypeStruct((n, d), BF16)]),
        compiler_params=_cparams(("parallel",)),
        name="out_proj",
    )(x2d, merged, w, g)


_SPLITS = (A_WIDTH, KV_RANK, IDX_HEADS * IDX_DIM, IDX_DIM, IDX_HEADS, A_WIDTH, B_WIDTH, B_WIDTH,
           B_WIDTH, 3 * C_WIDTH, C_HEADS, C_HEADS, C_WIDTH, N_BRANCH * D_MODEL)


def _repack_w_in(w):
    offs = np.concatenate([[0], np.cumsum(_SPLITS)])
    seg = [w[:, offs[i]:offs[i + 1]] for i in range(len(_SPLITS))]
    (q_a, c_kv, q_idx, k_idx, idx_w, z_a, glu_v, glu_g, z_b, qkv, beta, alpha, z_c, gates) = seg
    big = jnp.concatenate([q_a, q_idx, z_a, glu_v, glu_g, z_b, z_c, gates, c_kv], axis=1)
    pad = jnp.zeros((w.shape[0], SM_COLS - (SM_ALPHA + C_HEADS)), w.dtype)
    small = jnp.concatenate([k_idx, idx_w, beta, alpha, pad], axis=1)
    return big.astype(BF16), small.astype(BF16), qkv.astype(BF16)


def _pad_lanes(vec, off):
    out = jnp.zeros((1, SM_COLS), F32)
    return out.at[0, off:off + vec.shape[0]].set(vec.astype(F32))


def _pick(total, pref):
    for c in pref:
        if total % c == 0:
            return c
    return total


def _tiles(n, t):
    return dict(tm_proj=t, tq=_pick(t, (256, 128)),
                tk=_pick(t, (256, 128)), tt=_pick(t, (512, 256, 128)),
                tm=_pick(n, (512, 256)), tm_merge=_pick(n, (1024, 512, 256)), hb=8,
                tb=_pick(t, (512, 256, 128)))


def kernel(x, norm_g, w_in, kv_norm_g, idx_k_norm_g, w_uk, w_uv, w_proj_a, conv_w_b, conv_b_b,
           ln_g_b, ln_b_b, w_proj_b, conv_w_c, a_log, dt_bias, onorm_g_c, w_proj_c, w_out, final_g):
    b, t, _ = x.shape
    return _forward(_tiles(b * t, t), x, norm_g, w_in, kv_norm_g, idx_k_norm_g, w_uk, w_uv, w_proj_a,
                    conv_w_b, conv_b_b, ln_g_b, ln_b_b, w_proj_b, conv_w_c, a_log, dt_bias,
                    onorm_g_c, w_proj_c, w_out, final_g)


def _forward(tiles, x, norm_g, w_in, kv_norm_g, idx_k_norm_g, w_uk, w_uv, w_proj_a, conv_w_b,
             conv_b_b, ln_g_b, ln_b_b, w_proj_b, conv_w_c, a_log, dt_bias, onorm_g_c, w_proj_c,
             w_out, final_g):
    b, t, d = x.shape
    n = b * t
    depth = norm_g.shape[0]
    topk = min(TOPK_MAX, t // 4)
    nc = t // CHUNK
    tm_proj, tq, tk, tt, tm, tm_merge, hb, tb = (
        tiles[k] for k in ("tm_proj", "tq", "tk", "tt", "tm", "tm_merge", "hb", "tb"))

    x2d = x.reshape(n, d)
    xn = _rmsnorm(x2d, norm_g[0][None, :].astype(F32), tm)
    for l in range(depth):
        w_big, w_small, w_qkv = _repack_w_in(w_in[l])
        p, sm = _proj(xn, w_big, w_small, tm_proj, 512)
        p3 = p.reshape(b, t, P_COLS)
        sm3 = sm.reshape(b, t, SM_COLS)

        kmat, vmat_t = _kv_expand(p3, kv_norm_g[l][None, :].astype(F32),
                                  w_uk[l].reshape(KV_RANK, A_WIDTH).astype(BF16),
                                  w_uv[l].reshape(KV_RANK, A_WIDTH).T.astype(BF16), tk)
        u_a = _dsa(p3, sm3, kmat, vmat_t, _pad_lanes(idx_k_norm_g[l], SM_KIDX), tq, tk, topk)

        cw_b = jnp.concatenate(
            [conv_w_b[l].astype(F32), jnp.zeros((CONF_HALO - CONV_K, B_WIDTH), F32)], axis=0)
        u_b = _conformer(p3, cw_b, conv_b_b[l][None, :].astype(F32), ln_g_b[l][None, :].astype(F32),
                         ln_b_b[l][None, :].astype(F32), tt)

        cw_c = jnp.concatenate(
            [conv_w_c[l].astype(F32), jnp.zeros((SUBLANES - SHORT_K, 3 * C_WIDTH), F32)], axis=0)
        assert tm_proj % t == 0, "a projection row tile must hold whole sequences"
        qkv3 = _qkv_proj(xn, w_qkv, cw_c, tm_proj, 512, t).reshape(b, t, 3 * C_WIDTH)
        gate = _gdn_gate(sm3, _pad_lanes(a_log[l], SM_ALPHA), _pad_lanes(dt_bias[l], SM_ALPHA))
        beta = jnp.transpose(gate[:, :, SM_BETA:SM_BETA + C_HEADS], (0, 2, 1))
        gcum = jnp.transpose(gate[:, :, SM_ALPHA:SM_ALPHA + C_HEADS], (0, 2, 1))
        col = lambda a: jnp.transpose(a.reshape(b, C_HEADS // hb, hb, t), (0, 1, 3, 2))
        u_c = _gdn(qkv3, p3, col(beta), col(gcum),
                   gcum.reshape(b, C_HEADS, nc, 1, CHUNK), onorm_g_c[l][None, :].astype(F32), hb, tb)

        merged = _merge(u_a.reshape(n, A_WIDTH), u_b.reshape(n, B_WIDTH), u_c.reshape(n, C_WIDTH), p,
                        w_proj_a[l].astype(BF16), w_proj_b[l].astype(BF16), w_proj_c[l].astype(BF16),
                        tm_merge, 512)
        if l == depth - 1:
            x2d = _out_proj(x2d, merged, w_out[l].astype(BF16), final_g[None, :].astype(F32), tm, True)
        else:
            x2d, xn = _out_proj(x2d, merged, w_out[l].astype(BF16),
                                norm_g[l + 1][None, :].astype(F32), tm, False)
    return x2d.reshape(b, t, d)
```

```python
import functools
import math

import jax
import jax.numpy as jnp
import numpy as np
from jax import lax
from jax.experimental import pallas as pl
from jax.experimental.pallas import tpu as pltpu

F32 = jnp.float32
BF16 = jnp.bfloat16

D_MODEL = 2048
NORM_EPS = 1e-6
LN_EPS = 1e-5
L2_EPS = 1e-6
A_HEADS = 8
A_HEAD_DIM = 128
A_WIDTH = A_HEADS * A_HEAD_DIM
KV_RANK = 512
IDX_HEADS = 16
IDX_DIM = 64
TOPK_MAX = 256
B_WIDTH = 1024
CONV_K = 31
C_HEADS = 16
C_HEAD_DIM = 128
C_WIDTH = C_HEADS * C_HEAD_DIM
SHORT_K = 4
CHUNK = 64
N_BRANCH = 3

LANES = 128
SUBLANES = 8
VMEM_LIMIT = 56 * 1024 * 1024

OFF_QA = 0
OFF_QIDX = OFF_QA + A_WIDTH
OFF_ZA = OFF_QIDX + IDX_HEADS * IDX_DIM
OFF_ZC = OFF_ZA + A_WIDTH
OFF_GATE = OFF_ZC + C_WIDTH
OFF_CKV = OFF_GATE + N_BRANCH * D_MODEL
P_COLS = OFF_CKV + KV_RANK
SM_KIDX = 0
SM_IDXW = SM_KIDX + IDX_DIM
SM_BETA = SM_IDXW + IDX_HEADS
SM_ALPHA = SM_BETA + C_HEADS
SM_COLS = LANES

INT_MIN = -(2**31)


def _cparams(sem):
    return pltpu.CompilerParams(dimension_semantics=sem, vmem_limit_bytes=VMEM_LIMIT)


def _sigmoid(x):
    return 1.0 / (1.0 + jnp.exp(-x))


def _silu(x):
    return x * _sigmoid(x)


def _dot(a, b):
    return jnp.dot(a, b, preferred_element_type=F32)


def _dot_nt(a, b):
    return lax.dot_general(a, b, (((1,), (1,)), ((), ())), preferred_element_type=F32)


def _dot_tn(a, b):
    return lax.dot_general(a, b, (((0,), (0,)), ((), ())), preferred_element_type=F32)


def _bdot(a, b):
    return lax.dot_general(a, b, (((2,), (1,)), ((0,), (0,))), preferred_element_type=F32)


def _bdot_tn(a, b):
    return lax.dot_general(a, b, (((1,), (1,)), ((0,), (0,))), preferred_element_type=F32)


def _bdot_nt(a, b):
    return lax.dot_general(a, b, (((2,), (2,)), ((0,), (0,))), preferred_element_type=F32)


def _rmsnorm_rows(x, g):
    ms = jnp.mean(x * x, axis=-1, keepdims=True)
    return x * lax.rsqrt(ms + NORM_EPS) * g


def _rms_kernel(x_ref, g_ref, o_ref):
    o_ref[...] = _rmsnorm_rows(x_ref[...], g_ref[...]).astype(o_ref.dtype)


def _rmsnorm(x2d, g, tm):
    n, d = x2d.shape
    return pl.pallas_call(
        _rms_kernel,
        grid=(n // tm,),
        in_specs=[pl.BlockSpec((tm, d), lambda i: (i, 0)), pl.BlockSpec((1, d), lambda i: (0, 0))],
        out_specs=pl.BlockSpec((tm, d), lambda i: (i, 0)),
        out_shape=jax.ShapeDtypeStruct((n, d), BF16),
        compiler_params=_cparams(("parallel",)),
        name="rmsnorm",
    )(x2d, g)


def _proj_kernel(xn_ref, w_ref, ws_ref, o_ref, os_ref):
    xn = xn_ref[...]
    o_ref[...] = _dot(xn, w_ref[...]).astype(o_ref.dtype)

    @pl.when(pl.program_id(1) == 0)
    def _():
        os_ref[...] = _dot(xn, ws_ref[...])


def _proj(xn, w, w_small, tm, tn):
    n, d = xn.shape
    cols = w.shape[1]
    return pl.pallas_call(
        _proj_kernel,
        grid=(n // tm, cols // tn),
        in_specs=[
            pl.BlockSpec((tm, d), lambda i, j: (i, 0)),
            pl.BlockSpec((d, tn), lambda i, j: (0, j)),
            pl.BlockSpec((d, SM_COLS), lambda i, j: (0, 0)),
        ],
        out_specs=[pl.BlockSpec((tm, tn), lambda i, j: (i, j)),
                   pl.BlockSpec((tm, SM_COLS), lambda i, j: (i, 0))],
        out_shape=[jax.ShapeDtypeStruct((n, cols), BF16), jax.ShapeDtypeStruct((n, SM_COLS), F32)],
        compiler_params=_cparams(("parallel", "arbitrary")),
        name="in_proj",
    )(xn, w, w_small)


POS_SPLIT = 64
POS_TERMS = 3


def _kv_kernel(c_ref, g_ref, wk_ref, wvt_ref, k_ref, vt_ref):
    tk = c_ref.shape[0]
    c = c_ref[...].astype(F32)
    ms = jnp.mean(c * c, axis=-1, keepdims=True)
    cn = (c * lax.rsqrt(ms + NORM_EPS) * g_ref[...]).astype(BF16)
    k = _dot(cn, wk_ref[...]).astype(BF16)
    vt_ref[...] = _dot_nt(wvt_ref[...], cn).astype(BF16)
    pos = pl.program_id(1) * tk + lax.broadcasted_iota(jnp.int32, (tk, A_HEAD_DIM), 0)
    lane = lax.broadcasted_iota(jnp.int32, (tk, A_HEAD_DIM), 1)
    feat = jnp.where(lane < POS_TERMS, pos // POS_SPLIT,
                     jnp.where(lane < 2 * POS_TERMS, pos % POS_SPLIT, 0)).astype(F32).astype(BF16)
    for h in range(A_HEADS):
        k_ref[:, 2 * h * A_HEAD_DIM:(2 * h + 1) * A_HEAD_DIM] = k[:, h * A_HEAD_DIM:(h + 1) * A_HEAD_DIM]
        k_ref[:, (2 * h + 1) * A_HEAD_DIM:(2 * h + 2) * A_HEAD_DIM] = feat


def _kv_expand(p3, g, wk, wvt, tk):
    b, t, _ = p3.shape
    return pl.pallas_call(
        _kv_kernel,
        grid=(b, t // tk),
        in_specs=[
            pl.BlockSpec((None, tk, KV_RANK), lambda i, j: (i, j, OFF_CKV // KV_RANK)),
            pl.BlockSpec((1, KV_RANK), lambda i, j: (0, 0)),
            pl.BlockSpec((KV_RANK, A_WIDTH), lambda i, j: (0, 0)),
            pl.BlockSpec((A_WIDTH, KV_RANK), lambda i, j: (0, 0)),
        ],
        out_specs=[
            pl.BlockSpec((None, tk, 2 * A_WIDTH), lambda i, j: (i, j, 0)),
            pl.BlockSpec((None, None, A_WIDTH, tk), lambda i, j: (i, j, 0, 0)),
        ],
        out_shape=[jax.ShapeDtypeStruct((b, t, 2 * A_WIDTH), BF16),
                   jax.ShapeDtypeStruct((b, t // tk, A_WIDTH, tk), BF16)],
        compiler_params=_cparams(("parallel", "parallel")),
        name="kv_expand",
    )(p3, g, wk, wvt)


LOG2E = 1.4426950408889634
MASK_BIAS = -1e30
M_INIT = -1e29
CNT_ROWS = 32
SEL_UNROLL = 4


def _dsa_kernel(qa_ref, qi_ref, za_ref, smq_ref, smk_ref, gk_ref, k_ref, vt_ref, o_ref,
                kn_ref, sc_ref, qs_ref, m_ref, l_ref, acc_ref, s_ref, *, tq, tk, topk):
    qb = pl.program_id(1)

    @pl.when(qb == 0)
    def _():
        kx = smk_ref[...]
        lane = lax.broadcasted_iota(jnp.int32, (1, SM_COLS), 1)
        is_k = lane < IDX_DIM
        kx = jnp.where(is_k, kx, 0.0)
        ms = jnp.sum(kx * kx, axis=-1, keepdims=True) * (1.0 / IDX_DIM)
        kn = kx * lax.rsqrt(ms + NORM_EPS) * gk_ref[...]
        kn_ref[0] = kn.astype(BF16)
        kn_ref[1] = pltpu.roll(kn, IDX_DIM, 1).astype(BF16)

    nkc = (qb * tq + tq + tk - 1) // tk
    qpos = qb * tq + lax.broadcasted_iota(jnp.int32, (1, tq), 1)
    kpos0 = lax.broadcasted_iota(jnp.int32, (tk, 1), 0)
    w_t = smq_ref[...].T * (IDX_HEADS ** -0.5 * IDX_DIM ** -0.5)

    def score_chunk(c, carry):
        k0 = pl.multiple_of(c * tk, tk)
        acc = jnp.zeros((tk, tq), F32)
        for hp in range(IDX_HEADS // 2):
            qpair = qi_ref[:, hp * LANES:(hp + 1) * LANES]
            for half in range(2):
                row = SM_IDXW + 2 * hp + half
                kc = kn_ref[half, pl.ds(k0, tk), :]
                acc = acc + w_t[row:row + 1, :] * jnp.maximum(_dot_nt(kc, qpair), 0.0)
        bits = pltpu.bitcast(acc, jnp.int32)
        skey = jnp.where(bits < 0, bits ^ jnp.int32(0x7FFFFFFF), bits)
        sc_ref[c] = jnp.where(k0 + kpos0 <= qpos, skey, jnp.int32(INT_MIN))
        return carry

    lax.fori_loop(0, nkc, score_chunk, 0)

    def count_ge(cand):
        def body(c, acc):
            ind = jnp.where(sc_ref[c] >= cand, 1.0, 0.0)
            return acc + jnp.sum(ind.reshape(tk // CNT_ROWS, CNT_ROWS, tq), axis=0)

        acc = lax.fori_loop(0, nkc, body, jnp.zeros((CNT_ROWS, tq), F32))
        return jnp.sum(acc, axis=0, keepdims=True)

    fkeep = jnp.float32(topk)

    def sel_cond(st):
        i, _, done = st
        return (i < 32) & (jnp.min(done) < 0.5)

    def sel_step(st):
        i, prefix, done = st
        for u in range(SEL_UNROLL):
            cand = prefix ^ (jnp.int32(1) << (31 - u - i))
            cnt = count_ge(cand)
            prefix = jnp.where((done < 0.5) & (cnt >= fkeep), cand, prefix)
            done = jnp.where(cnt == fkeep, 1.0, done)
        return i + SEL_UNROLL, prefix, done

    done0 = jnp.where(qpos < topk, 1.0, 0.0)
    _, kth, _ = lax.while_loop(
        sel_cond, sel_step, (jnp.int32(0), jnp.full((1, tq), INT_MIN, jnp.int32), done0))
    thr = jnp.maximum(kth, jnp.int32(INT_MIN + 1))

    lane = lax.broadcasted_iota(jnp.int32, (1, A_HEAD_DIM), 1)
    for h in range(A_HEADS):
        slope = 2.0 ** (-8.0 * (h + 1) / A_HEADS) * LOG2E
        row = jnp.zeros((1, A_HEAD_DIM), F32)
        rest = np.float32(slope)
        for i in range(POS_TERMS):
            piece = np.float32(rest.astype(BF16))
            rest = np.float32(rest - piece)
            row = jnp.where(lane == i, float(piece) * POS_SPLIT, row)
            row = jnp.where(lane == POS_TERMS + i, float(piece), row)
        hs = slice(h * A_HEAD_DIM, (h + 1) * A_HEAD_DIM)
        qs_ref[:, 2 * h * A_HEAD_DIM:(2 * h + 1) * A_HEAD_DIM] = (
            qa_ref[:, hs].astype(F32) * (A_HEAD_DIM ** -0.5 * LOG2E)).astype(BF16)
        qs_ref[:, (2 * h + 1) * A_HEAD_DIM:(2 * h + 2) * A_HEAD_DIM] = jnp.broadcast_to(
            row, (tq, A_HEAD_DIM)).astype(BF16)
    m_ref[...] = jnp.full(m_ref.shape, M_INIT, F32)
    l_ref[...] = jnp.zeros(l_ref.shape, F32)
    acc_ref[...] = jnp.zeros(acc_ref.shape, F32)

    def qk_scores(c, slot):
        k0 = pl.multiple_of(c * tk, tk)
        for h in range(A_HEADS):
            hs2 = slice(2 * h * A_HEAD_DIM, (2 * h + 2) * A_HEAD_DIM)
            s_ref[slot, h] = _dot_nt(k_ref[pl.ds(k0, tk), hs2], qs_ref[:, hs2])

    def att_chunk(c, carry):
        slot = c % 2
        bias = jnp.where(sc_ref[c] >= thr, 0.0, MASK_BIAS)
        for h in range(A_HEADS):
            hs = slice(h * A_HEAD_DIM, (h + 1) * A_HEAD_DIM)
            s = s_ref[slot, h] + bias
            m_old = m_ref[h]
            m_new = jnp.maximum(m_old, jnp.max(s, axis=0, keepdims=True))
            p = jnp.exp2(s - m_new)
            alpha = jnp.exp2(m_old - m_new)
            l_ref[h] = alpha * l_ref[h] + jnp.sum(p, axis=0, keepdims=True)
            acc_ref[hs, :] = alpha * acc_ref[hs, :] + _dot(vt_ref[c, hs, :], p.astype(BF16))
            m_ref[h] = m_new
        qk_scores(jnp.minimum(c + 1, nkc - 1), 1 - slot)
        return carry

    qk_scores(0, 0)
    lax.fori_loop(0, nkc, att_chunk, 0)
    for h in range(A_HEADS):
        hs = slice(h * A_HEAD_DIM, (h + 1) * A_HEAD_DIM)
        z = za_ref[:, hs].astype(F32)
        o_ref[:, hs] = ((acc_ref[hs, :] / l_ref[h]).T * _silu(z)).astype(o_ref.dtype)


def _dsa(p3, sm3, kmat, vmat_t, gk, tq, tk, topk):
    b, t, _ = p3.shape
    kern = functools.partial(_dsa_kernel, tq=tq, tk=tk, topk=topk)
    return pl.pallas_call(
        kern,
        grid=(b, t // tq),
        in_specs=[
            pl.BlockSpec((None, tq, A_WIDTH), lambda i, j: (i, j, OFF_QA // A_WIDTH)),
            pl.BlockSpec((None, tq, A_WIDTH), lambda i, j: (i, j, OFF_QIDX // A_WIDTH)),
            pl.BlockSpec((None, tq, A_WIDTH), lambda i, j: (i, j, OFF_ZA // A_WIDTH)),
            pl.BlockSpec((None, tq, SM_COLS), lambda i, j: (i, j, 0)),
            pl.BlockSpec((None, t, SM_COLS), lambda i, j: (i, 0, 0)),
            pl.BlockSpec((1, SM_COLS), lambda i, j: (0, 0)),
            pl.BlockSpec((None, t, 2 * A_WIDTH), lambda i, j: (i, 0, 0)),
            pl.BlockSpec((None, t // tk, A_WIDTH, tk), lambda i, j: (i, 0, 0, 0)),
        ],
        out_specs=pl.BlockSpec((None, tq, A_WIDTH), lambda i, j: (i, j, 0)),
        out_shape=jax.ShapeDtypeStruct((b, t, A_WIDTH), BF16),
        scratch_shapes=[
            pltpu.VMEM((2, t, LANES), BF16),
            pltpu.VMEM((t // tk, tk, tq), jnp.int32),
            pltpu.VMEM((tq, 2 * A_WIDTH), BF16),
            pltpu.VMEM((A_HEADS, 1, tq), F32),
            pltpu.VMEM((A_HEADS, 1, tq), F32),
            pltpu.VMEM((A_WIDTH, tq), F32),
            pltpu.VMEM((2, A_HEADS, tk, tq), F32),
        ],
        compiler_params=_cparams(("parallel", "arbitrary")),
        name="dsa_attention",
    )(p3, p3, p3, sm3, sm3, gk, kmat, vmat_t)


CONF_HALO = 32
CONF_ROWS = 64
CONF_WIN = CONF_ROWS + CONF_HALO


CONF_BLK = 2 * CONF_ROWS


def _conf_kernel(xn_ref, w_ref, cw_ref, cb_ref, lg_ref, lb_ref, o_ref, y_ref, hp_ref, c_ref, z_ref,
                 *, seq):
    tm = o_ref.shape[0]
    nb = tm // CONF_BLK
    first = CONF_HALO - (CONV_K - 1)

    def project(r, slot):
        r0 = pl.multiple_of(r * CONF_BLK, CONF_BLK)
        y_ref[slot] = _dot(xn_ref[pl.ds(r0, CONF_BLK), :], w_ref[...])

    def epilogue(r, slot):
        r0 = pl.multiple_of(r * CONF_BLK, CONF_BLK)
        glu = y_ref[slot, :, 0:B_WIDTH] * _sigmoid(y_ref[slot, :, B_WIDTH:2 * B_WIDTH])
        z_ref[...] = _silu(y_ref[slot, :, 2 * B_WIDTH:3 * B_WIDTH])
        history = hp_ref[CONF_BLK:CONF_BLK + CONF_HALO, :]
        hp_ref[0:CONF_HALO, :] = jnp.where(r % (seq // CONF_BLK) == 0, 0.0, history)
        hp_ref[CONF_HALO:CONF_HALO + CONF_BLK, :] = glu
        for ct in range(B_WIDTH // LANES):
            cs = slice(ct * LANES, (ct + 1) * LANES)
            for base in range(0, CONF_BLK, CONF_ROWS):
                win = hp_ref[base:base + CONF_WIN, cs]
                acc = jnp.zeros((CONF_ROWS, LANES), F32)
                for rem in range(SUBLANES):
                    rot = win if rem == 0 else pltpu.roll(win, CONF_WIN - rem, 0)
                    for j in range(CONV_K):
                        if (first + j) % SUBLANES == rem:
                            a = (first + j) - rem
                            acc = acc + cw_ref[j:j + 1, cs] * rot[a:a + CONF_ROWS]
                c_ref[base:base + CONF_ROWS, cs] = acc + cb_ref[:, cs]
        y = c_ref[...]
        mu = jnp.mean(y, axis=-1, keepdims=True)
        yc = y - mu
        var = jnp.mean(yc * yc, axis=-1, keepdims=True)
        yn = yc * lax.rsqrt(var + LN_EPS) * lg_ref[...] + lb_ref[...]
        o_ref[pl.ds(r0, CONF_BLK), :] = (_silu(yn) * z_ref[...]).astype(o_ref.dtype)

    def step(r, carry):
        slot = r % 2
        epilogue(r, slot)
        project(jnp.minimum(r + 1, nb - 1), 1 - slot)
        return carry

    hp_ref[...] = jnp.zeros(hp_ref.shape, F32)
    project(0, 0)
    lax.fori_loop(0, nb, step, 0)


def _conformer(xn, w, cw, cb, lg, lb, tm, seq):
    n, d = xn.shape
    kern = functools.partial(_conf_kernel, seq=seq)
    once = dict(pipeline_mode=pl.Buffered(1))
    vec = lambda r: pl.BlockSpec((r, B_WIDTH), lambda i: (0, 0), **once)
    return pl.pallas_call(
        kern,
        grid=(n // tm,),
        in_specs=[pl.BlockSpec((tm, d), lambda i: (i, 0)),
                  pl.BlockSpec((d, 3 * B_WIDTH), lambda i: (0, 0), **once),
                  vec(CONF_HALO), vec(1), vec(1), vec(1)],
        out_specs=pl.BlockSpec((tm, B_WIDTH), lambda i: (i, 0)),
        out_shape=jax.ShapeDtypeStruct((n, B_WIDTH), BF16),
        scratch_shapes=[pltpu.VMEM((2, CONF_BLK, 3 * B_WIDTH), F32),
                        pltpu.VMEM((CONF_HALO + CONF_BLK, B_WIDTH), F32),
                        pltpu.VMEM((CONF_BLK, B_WIDTH), F32),
                        pltpu.VMEM((CONF_BLK, B_WIDTH), F32)],
        compiler_params=_cparams(("parallel",)),
        name="conformer_proj",
    )(xn, w, cw, cb, lg, lb)


GDN_HALO = 8
QKV_ROWS = 128


def _qkv_kernel(xn_ref, w_ref, cw_ref, o_ref, *, seq):
    j = pl.program_id(1)
    tm, tn = o_ref.shape
    first = GDN_HALO - (SHORT_K - 1)
    tiles_per_part = C_WIDTH // tn
    is_v = j >= 2 * tiles_per_part
    qscale = jnp.where(j < tiles_per_part, C_HEAD_DIM ** -0.5, 1.0)
    rb = min(QKV_ROWS, seq)
    for s0 in range(0, tm, seq):
        tail = jnp.zeros((GDN_HALO, tn), F32)
        for r0 in range(s0, s0 + seq, rb):
            y = _dot(xn_ref[r0:r0 + rb, :], w_ref[...])
            xp = jnp.concatenate([tail, y], axis=0)
            tail = y[rb - GDN_HALO:, :]
            for ct in range(tn // LANES):
                cs = slice(ct * LANES, (ct + 1) * LANES)
                win = xp[:, cs]
                acc = cw_ref[SHORT_K - 1:SHORT_K, cs] * win[GDN_HALO:]
                for back in range(1, SHORT_K):
                    tap = SHORT_K - 1 - back
                    acc = acc + cw_ref[tap:tap + 1, cs] * pltpu.roll(win, back, 0)[GDN_HALO:]
                a = _silu(acc)
                rs = lax.rsqrt(jnp.sum(a * a, axis=-1, keepdims=True) + L2_EPS)
                o_ref[r0:r0 + rb, cs] = (a * (jnp.where(is_v, 1.0, rs) * qscale)).astype(BF16)


def _qkv_proj(xn, w, cw, tm, tn, seq):
    n, d = xn.shape
    cols = w.shape[1]
    kern = functools.partial(_qkv_kernel, seq=seq)
    return pl.pallas_call(
        kern,
        grid=(n // tm, cols // tn),
        in_specs=[
            pl.BlockSpec((tm, d), lambda i, j: (i, 0)),
            pl.BlockSpec((d, tn), lambda i, j: (0, j)),
            pl.BlockSpec((SUBLANES, tn), lambda i, j: (0, j)),
        ],
        out_specs=pl.BlockSpec((tm, tn), lambda i, j: (i, j)),
        out_shape=jax.ShapeDtypeStruct((n, cols), BF16),
        compiler_params=_cparams(("parallel", "arbitrary")),
        name="qkv_proj",
    )(xn, w, cw)


def _gdn_gate_kernel(sm_ref, a_ref, dt_ref, o_ref):
    t = sm_ref.shape[0]
    x = sm_ref[...]
    lane = lax.broadcasted_iota(jnp.int32, (1, SM_COLS), 1)
    beta = _sigmoid(x)
    y = x + dt_ref[...]
    softplus = jnp.maximum(y, 0.0) + jnp.log1p(jnp.exp(-jnp.abs(y)))
    g = -jnp.exp(a_ref[...]) * softplus
    row = lax.broadcasted_iota(jnp.int32, (CHUNK, CHUNK), 0)
    col = lax.broadcasted_iota(jnp.int32, (CHUNK, CHUNK), 1)
    tril = (row >= col).astype(F32)
    is_beta = (lane >= SM_BETA) & (lane < SM_BETA + C_HEADS)
    for c in range(t // CHUNK):
        rs = slice(c * CHUNK, (c + 1) * CHUNK)
        gc = jnp.dot(tril, g[rs], preferred_element_type=F32, precision=lax.Precision.HIGHEST)
        o_ref[rs, :] = jnp.where(is_beta, beta[rs], gc)


def _gdn_gate(sm3, a_vec, dt_vec):
    b, t, _ = sm3.shape
    return pl.pallas_call(
        _gdn_gate_kernel,
        grid=(b,),
        in_specs=[
            pl.BlockSpec((None, t, SM_COLS), lambda i: (i, 0, 0)),
            pl.BlockSpec((1, SM_COLS), lambda i: (0, 0)),
            pl.BlockSpec((1, SM_COLS), lambda i: (0, 0)),
        ],
        out_specs=pl.BlockSpec((None, t, SM_COLS), lambda i: (i, 0, 0)),
        out_shape=jax.ShapeDtypeStruct((b, t, SM_COLS), F32),
        compiler_params=_cparams(("parallel",)),
        name="gdn_gate",
    )(sm3, a_vec, dt_vec)


GDN_BATCH = 32


def _unit_lower_inverse(m, idx_r, idx_c):
    n = m.shape[-1]
    eye = (idx_r == idx_c).astype(F32)
    blk = lambda s: (idx_r // s) == (idx_c // s)
    m0 = jnp.where(blk(8), m, 0.0)
    m0b = m0.astype(BF16)
    a2 = _bdot(m0b, m0b)
    p = eye - m0
    p = p + _bdot(p.astype(BF16), a2.astype(BF16))
    a4 = _bdot(a2.astype(BF16), a2.astype(BF16))
    p = p + _bdot(p.astype(BF16), a4.astype(BF16))
    s = 8
    while s < n:
        lo = jnp.where(blk(2 * s) & jnp.logical_not(blk(s)), m, 0.0)
        pb = p.astype(BF16)
        p = p - _bdot(pb, _bdot(lo.astype(BF16), pb).astype(BF16))
        s *= 2
    return p


def _gdn_kernel(q_ref, k_ref, v_ref, z_ref, bcol_ref, gcol_ref, grow_ref, on_ref, o_ref,
                a_ref, b_ref, qp_ref, op_ref, s_ref, eg_ref, st_ref, *, hb, hg):
    tb = q_ref.shape[0]
    nc = tb // CHUNK
    idx_r = lax.broadcasted_iota(jnp.int32, (1, CHUNK, CHUNK), 1)
    idx_c = lax.broadcasted_iota(jnp.int32, (1, CHUNK, CHUNK), 2)
    causal = idx_r >= idx_c
    strict = idx_r > idx_c

    @pl.when(pl.program_id(2) == 0)
    def _():
        st_ref[...] = jnp.zeros(st_ref.shape, F32)

    heads = lambda ref: lambda h: ref[:, h * C_HEAD_DIM:(h + 1) * C_HEAD_DIM].astype(F32).reshape(
        nc, CHUNK, C_HEAD_DIM)
    cols = lambda ref: lambda h: ref[:, h:h + 1].reshape(nc, CHUNK, 1)
    for h0 in range(0, hb, hg):
        stack = lambda f: jnp.concatenate([f(h) for h in range(h0, h0 + hg)], axis=0)
        hsl = slice(h0, h0 + hg)
        q = stack(heads(q_ref))
        k = stack(heads(k_ref))
        v = stack(heads(v_ref))
        beta = stack(cols(bcol_ref))
        gc = stack(cols(gcol_ref))
        gr = stack(lambda h: grow_ref[h])
        decay = jnp.exp(jnp.where(causal, gc - gr, -jnp.inf))
        kb = k * beta
        kbf = k.astype(BF16)
        m = jnp.where(strict, _bdot_nt(kb.astype(BF16), kbf) * decay, 0.0)
        tinv = _unit_lower_inverse(m, idx_r, idx_c)
        rhs = jnp.concatenate([v * beta, kb * jnp.exp(gc)], axis=-1)
        sol = _bdot(tinv.astype(BF16), rhs.astype(BF16)).astype(BF16)
        ub = sol[:, :, :C_HEAD_DIM]
        wb = sol[:, :, C_HEAD_DIM:]
        glast = gc[:, CHUNK - 1:CHUNK, :]
        kd = (k * jnp.exp(glast - gc)).astype(BF16)
        aqk = (_bdot_nt(q.astype(BF16), kbf) * decay).astype(BF16)
        unstack = lambda a: a.reshape((hg, nc) + a.shape[1:])
        a_ref[hsl] = unstack((-_bdot_tn(kd, wb)).astype(BF16))
        b_ref[hsl] = unstack(_bdot_tn(kd, ub).astype(BF16))
        qp_ref[hsl] = unstack((q * jnp.exp(gc) - _bdot(aqk, wb)).astype(BF16))
        op_ref[hsl] = unstack(_bdot(aqk, ub))
        eg_ref[hsl] = unstack(jnp.broadcast_to(jnp.exp(glast), (hg * nc, 1, C_HEAD_DIM)))

    def step(c, carry):
        for h in range(hb):
            s = st_ref[h]
            sb = s.astype(BF16)
            s_ref[h, c] = sb
            st_ref[h] = s * eg_ref[h, c] + _dot(a_ref[h, c], sb) + b_ref[h, c].astype(F32)
        return carry

    lax.fori_loop(0, nc, step, 0)

    for h in range(hb):
        hs = slice(h * C_HEAD_DIM, (h + 1) * C_HEAD_DIM)
        o = (_bdot(qp_ref[h], s_ref[h]) + op_ref[h]).reshape(tb, C_HEAD_DIM)
        ms = jnp.mean(o * o, axis=-1, keepdims=True)
        on = o * lax.rsqrt(ms + NORM_EPS) * on_ref[...]
        z = z_ref[:, hs].astype(F32)
        o_ref[:, hs] = (on * _silu(z)).astype(o_ref.dtype)


def _gdn(qkv3, p3, bcol, gcol, grow, on_g, hb, tb):
    b, t, _ = qkv3.shape
    nc = tb // CHUNK
    wd = hb * C_HEAD_DIM
    hg = min(hb, max(1, GDN_BATCH // nc))
    kern = functools.partial(_gdn_kernel, hb=hb, hg=hg)
    part = lambda which: pl.BlockSpec(
        (None, tb, wd), lambda i, j, s: (i, s, which * (C_WIDTH // wd) + j))
    return pl.pallas_call(
        kern,
        grid=(b, C_HEADS // hb, t // tb),
        in_specs=[
            part(0), part(1), part(2),
            pl.BlockSpec((None, tb, wd), lambda i, j, s: (i, s, OFF_ZC // wd + j)),
            pl.BlockSpec((None, None, tb, hb), lambda i, j, s: (i, j, s, 0)),
            pl.BlockSpec((None, None, tb, hb), lambda i, j, s: (i, j, s, 0)),
            pl.BlockSpec((None, hb, nc, 1, CHUNK), lambda i, j, s: (i, j, s, 0, 0)),
            pl.BlockSpec((1, C_HEAD_DIM), lambda i, j, s: (0, 0)),
        ],
        out_specs=pl.BlockSpec((None, tb, wd), lambda i, j, s: (i, s, j)),
        out_shape=jax.ShapeDtypeStruct((b, t, C_WIDTH), BF16),
        scratch_shapes=[
            pltpu.VMEM((hb, nc, C_HEAD_DIM, C_HEAD_DIM), BF16),
            pltpu.VMEM((hb, nc, C_HEAD_DIM, C_HEAD_DIM), BF16),
            pltpu.VMEM((hb, nc, CHUNK, C_HEAD_DIM), BF16),
            pltpu.VMEM((hb, nc, CHUNK, C_HEAD_DIM), F32),
            pltpu.VMEM((hb, nc, C_HEAD_DIM, C_HEAD_DIM), BF16),
            pltpu.VMEM((hb, nc, 1, C_HEAD_DIM), F32),
            pltpu.VMEM((hb, C_HEAD_DIM, C_HEAD_DIM), F32),
        ],
        compiler_params=_cparams(("parallel", "parallel", "arbitrary")),
        name="gdn_delta_rule",
    )(qkv3, qkv3, qkv3, p3, bcol, gcol, grow, on_g)


def _merge_kernel(ua_ref, ub_ref, uc_ref, g0_ref, g1_ref, g2_ref, wa_ref, wb_ref, wc_ref, o_ref):
    ya = _dot(ua_ref[...], wa_ref[...])
    yb = _dot(ub_ref[...], wb_ref[...])
    yc = _dot(uc_ref[...], wc_ref[...])
    o = (_sigmoid(g0_ref[...].astype(F32)) * ya + _sigmoid(g1_ref[...].astype(F32)) * yb
         + _sigmoid(g2_ref[...].astype(F32)) * yc)
    o_ref[...] = o.astype(o_ref.dtype)


def _merge(ua, ub, uc, p, wa, wb, wc, tm, tn):
    n = ua.shape[0]
    gate = lambda br: pl.BlockSpec((tm, tn), lambda i, j: (i, (OFF_GATE + br * D_MODEL) // tn + j))
    return pl.pallas_call(
        _merge_kernel,
        grid=(n // tm, D_MODEL // tn),
        in_specs=[
            pl.BlockSpec((tm, A_WIDTH), lambda i, j: (i, 0)),
            pl.BlockSpec((tm, B_WIDTH), lambda i, j: (i, 0)),
            pl.BlockSpec((tm, C_WIDTH), lambda i, j: (i, 0)),
            gate(0), gate(1), gate(2),
            pl.BlockSpec((A_WIDTH, tn), lambda i, j: (0, j)),
            pl.BlockSpec((B_WIDTH, tn), lambda i, j: (0, j)),
            pl.BlockSpec((C_WIDTH, tn), lambda i, j: (0, j)),
        ],
        out_specs=pl.BlockSpec((tm, tn), lambda i, j: (i, j)),
        out_shape=jax.ShapeDtypeStruct((n, D_MODEL), BF16),
        compiler_params=_cparams(("parallel", "arbitrary")),
        name="gate_merge",
    )(ua, ub, uc, p, p, p, wa, wb, wc)


def _out_kernel(x_ref, m_ref, w_ref, g_ref, *o_refs, final):
    y = x_ref[...] + _dot(m_ref[...], w_ref[...])
    if final:
        o_refs[0][...] = _rmsnorm_rows(y, g_ref[...])
    else:
        o_refs[0][...] = y
        o_refs[1][...] = _rmsnorm_rows(y, g_ref[...]).astype(BF16)


def _out_proj(x2d, merged, w, g, tm, final):
    n, d = x2d.shape
    kern = functools.partial(_out_kernel, final=final)
    row = pl.BlockSpec((tm, d), lambda i: (i, 0))
    return pl.pallas_call(
        kern,
        grid=(n // tm,),
        in_specs=[row, row, pl.BlockSpec((d, d), lambda i: (0, 0)), pl.BlockSpec((1, d), lambda i: (0, 0))],
        out_specs=row if final else [row, row],
        out_shape=(jax.ShapeDtypeStruct((n, d), F32) if final else
                   [jax.ShapeDtypeStruct((n, d), F32), jax.ShapeDtypeStruct((n, d), BF16)]),
        compiler_params=_cparams(("parallel",)),
        name="out_proj",
    )(x2d, merged, w, g)


_SPLITS = (A_WIDTH, KV_RANK, IDX_HEADS * IDX_DIM, IDX_DIM, IDX_HEADS, A_WIDTH, B_WIDTH, B_WIDTH,
           B_WIDTH, 3 * C_WIDTH, C_HEADS, C_HEADS, C_WIDTH, N_BRANCH * D_MODEL)


def _repack_w_in(w):
    offs = np.concatenate([[0], np.cumsum(_SPLITS)])
    seg = [w[:, offs[i]:offs[i + 1]] for i in range(len(_SPLITS))]
    (q_a, c_kv, q_idx, k_idx, idx_w, z_a, glu_v, glu_g, z_b, qkv, beta, alpha, z_c, gates) = seg
    big = jnp.concatenate([q_a, q_idx, z_a, z_c, gates, c_kv], axis=1)
    pad = jnp.zeros((w.shape[0], SM_COLS - (SM_ALPHA + C_HEADS)), w.dtype)
    small = jnp.concatenate([k_idx, idx_w, beta, alpha, pad], axis=1)
    glu = jnp.concatenate([glu_v, glu_g, z_b], axis=1)
    return big.astype(BF16), small.astype(BF16), qkv.astype(BF16), glu.astype(BF16)


def _pad_lanes(vec, off):
    out = jnp.zeros((1, SM_COLS), F32)
    return out.at[0, off:off + vec.shape[0]].set(vec.astype(F32))


def _pick(total, pref):
    for c in pref:
        if total % c == 0:
            return c
    return total


def _tiles(n, t):
    return dict(tm_proj=t, tq=_pick(t, (256, 128)),
                tk=_pick(t, (256, 128)), tt=_pick(t, (512, 256, 128)),
                tm=_pick(n, (512, 256)), tm_merge=_pick(n, (1024, 512, 256)), hb=8,
                tb=_pick(t, (512, 256, 128)))


def kernel(x, norm_g, w_in, kv_norm_g, idx_k_norm_g, w_uk, w_uv, w_proj_a, conv_w_b, conv_b_b,
           ln_g_b, ln_b_b, w_proj_b, conv_w_c, a_log, dt_bias, onorm_g_c, w_proj_c, w_out, final_g):
    b, t, _ = x.shape
    return _forward(_tiles(b * t, t), x, norm_g, w_in, kv_norm_g, idx_k_norm_g, w_uk, w_uv, w_proj_a,
                    conv_w_b, conv_b_b, ln_g_b, ln_b_b, w_proj_b, conv_w_c, a_log, dt_bias,
                    onorm_g_c, w_proj_c, w_out, final_g)


def _forward(tiles, x, norm_g, w_in, kv_norm_g, idx_k_norm_g, w_uk, w_uv, w_proj_a, conv_w_b,
             conv_b_b, ln_g_b, ln_b_b, w_proj_b, conv_w_c, a_log, dt_bias, onorm_g_c, w_proj_c,
             w_out, final_g):
    b, t, d = x.shape
    n = b * t
    depth = norm_g.shape[0]
    topk = min(TOPK_MAX, t // 4)
    nc = t // CHUNK
    tm_proj, tq, tk, tt, tm, tm_merge, hb, tb = (
        tiles[k] for k in ("tm_proj", "tq", "tk", "tt", "tm", "tm_merge", "hb", "tb"))

    x2d = x.reshape(n, d)
    xn = _rmsnorm(x2d, norm_g[0][None, :].astype(F32), tm)
    for l in range(depth):
        w_big, w_small, w_qkv, w_glu = _repack_w_in(w_in[l])
        assert tm_proj % t == 0, "a projection row tile must hold whole sequences"
        p, sm = _proj(xn, w_big, w_small, tm_proj, 512)
        p3 = p.reshape(b, t, P_COLS)
        sm3 = sm.reshape(b, t, SM_COLS)

        kmat, vmat_t = _kv_expand(p3, kv_norm_g[l][None, :].astype(F32),
                                  w_uk[l].reshape(KV_RANK, A_WIDTH).astype(BF16),
                                  w_uv[l].reshape(KV_RANK, A_WIDTH).T.astype(BF16), tk)
        u_a = _dsa(p3, sm3, kmat, vmat_t, _pad_lanes(idx_k_norm_g[l], SM_KIDX), tq, tk, topk)

        cw_b = jnp.concatenate(
            [conv_w_b[l].astype(F32), jnp.zeros((CONF_HALO - CONV_K, B_WIDTH), F32)], axis=0)
        u_b = _conformer(xn, w_glu, cw_b, conv_b_b[l][None, :].astype(F32),
                         ln_g_b[l][None, :].astype(F32), ln_b_b[l][None, :].astype(F32), tm_proj, t)

        cw_c = jnp.concatenate(
            [conv_w_c[l].astype(F32), jnp.zeros((SUBLANES - SHORT_K, 3 * C_WIDTH), F32)], axis=0)
        qkv3 =_qkv_proj(xn, w_qkv, cw_c, tm_proj, 512, t).reshape(b, t, 3 * C_WIDTH)
        gate = _gdn_gate(sm3, _pad_lanes(a_log[l], SM_ALPHA), _pad_lanes(dt_bias[l], SM_ALPHA))
        beta = jnp.transpose(gate[:, :, SM_BETA:SM_BETA + C_HEADS], (0, 2, 1))
        gcum = jnp.transpose(gate[:, :, SM_ALPHA:SM_ALPHA + C_HEADS], (0, 2, 1))
        col = lambda a: jnp.transpose(a.reshape(b, C_HEADS // hb, hb, t), (0, 1, 3, 2))
        u_c = _gdn(qkv3, p3, col(beta), col(gcum),
                   gcum.reshape(b, C_HEADS, nc, 1, CHUNK), onorm_g_c[l][None, :].astype(F32), hb, tb)

        merged = _merge(u_a.reshape(n, A_WIDTH), u_b.reshape(n, B_WIDTH), u_c.reshape(n, C_WIDTH), p,
                        w_proj_a[l].astype(BF16), w_proj_b[l].astype(BF16), w_proj_c[l].astype(BF16),
                        tm_merge, 512)
        if l == depth - 1:
            x2d = _out_proj(x2d, merged, w_out[l].astype(BF16), final_g[None, :].astype(F32), tm, True)
        else:
            x2d, xn = _out_proj(x2d, merged, w_out[l].astype(BF16),
                                norm_g[l + 1][None, :].astype(F32), tm, False)
    return x2d.reshape(b, t, d)
```

```python
import functools
import math

import jax
import jax.numpy as jnp
import numpy as np
from jax import lax
from jax.experimental import pallas as pl
from jax.experimental.pallas import tpu as pltpu

F32 = jnp.float32
BF16 = jnp.bfloat16

D_MODEL = 2048
NORM_EPS = 1e-6
LN_EPS = 1e-5
L2_EPS = 1e-6
A_HEADS = 8
A_HEAD_DIM = 128
A_WIDTH = A_HEADS * A_HEAD_DIM
KV_RANK = 512
IDX_HEADS = 16
IDX_DIM = 64
TOPK_MAX = 256
B_WIDTH = 1024
CONV_K = 31
C_HEADS = 16
C_HEAD_DIM = 128
C_WIDTH = C_HEADS * C_HEAD_DIM
SHORT_K = 4
CHUNK = 64
N_BRANCH = 3

LANES = 128
SUBLANES = 8
VMEM_LIMIT = 56 * 1024 * 1024

OFF_QA = 0
OFF_QIDX = OFF_QA + A_WIDTH
OFF_ZA = OFF_QIDX + IDX_HEADS * IDX_DIM
OFF_ZC = OFF_ZA + A_WIDTH
OFF_GATE = OFF_ZC + C_WIDTH
OFF_CKV = OFF_GATE + N_BRANCH * D_MODEL
P_COLS = OFF_CKV + KV_RANK
SM_KIDX = 0
SM_IDXW = SM_KIDX + IDX_DIM
SM_BETA = SM_IDXW + IDX_HEADS
SM_ALPHA = SM_BETA + C_HEADS
SM_COLS = LANES

INT_MIN = -(2**31)


def _cparams(sem):
    return pltpu.CompilerParams(dimension_semantics=sem, vmem_limit_bytes=VMEM_LIMIT)


def _sigmoid(x):
    return 1.0 / (1.0 + jnp.exp(-x))


def _silu(x):
    return x * _sigmoid(x)


def _dot(a, b):
    return jnp.dot(a, b, preferred_element_type=F32)


def _dot_nt(a, b):
    return lax.dot_general(a, b, (((1,), (1,)), ((), ())), preferred_element_type=F32)


def _dot_tn(a, b):
    return lax.dot_general(a, b, (((0,), (0,)), ((), ())), preferred_element_type=F32)


def _bdot(a, b):
    return lax.dot_general(a, b, (((2,), (1,)), ((0,), (0,))), preferred_element_type=F32)


def _bdot_tn(a, b):
    return lax.dot_general(a, b, (((1,), (1,)), ((0,), (0,))), preferred_element_type=F32)


def _bdot_nt(a, b):
    return lax.dot_general(a, b, (((2,), (2,)), ((0,), (0,))), preferred_element_type=F32)


def _rmsnorm_rows(x, g):
    ms = jnp.mean(x * x, axis=-1, keepdims=True)
    return x * lax.rsqrt(ms + NORM_EPS) * g


def _rms_kernel(x_ref, g_ref, o_ref):
    o_ref[...] = _rmsnorm_rows(x_ref[...], g_ref[...]).astype(o_ref.dtype)


def _rmsnorm(x2d, g, tm):
    n, d = x2d.shape
    return pl.pallas_call(
        _rms_kernel,
        grid=(n // tm,),
        in_specs=[pl.BlockSpec((tm, d), lambda i: (i, 0)), pl.BlockSpec((1, d), lambda i: (0, 0))],
        out_specs=pl.BlockSpec((tm, d), lambda i: (i, 0)),
        out_shape=jax.ShapeDtypeStruct((n, d), BF16),
        compiler_params=_cparams(("parallel",)),
        name="rmsnorm",
    )(x2d, g)


def _proj_kernel(xn_ref, w_ref, ws_ref, o_ref, os_ref):
    xn = xn_ref[...]
    o_ref[...] = _dot(xn, w_ref[...]).astype(o_ref.dtype)

    @pl.when(pl.program_id(1) == 0)
    def _():
        os_ref[...] = _dot(xn, ws_ref[...])


def _proj(xn, w, w_small, tm, tn):
    n, d = xn.shape
    cols = w.shape[1]
    return pl.pallas_call(
        _proj_kernel,
        grid=(n // tm, cols // tn),
        in_specs=[
            pl.BlockSpec((tm, d), lambda i, j: (i, 0)),
            pl.BlockSpec((d, tn), lambda i, j: (0, j)),
            pl.BlockSpec((d, SM_COLS), lambda i, j: (0, 0)),
        ],
        out_specs=[pl.BlockSpec((tm, tn), lambda i, j: (i, j)),
                   pl.BlockSpec((tm, SM_COLS), lambda i, j: (i, 0))],
        out_shape=[jax.ShapeDtypeStruct((n, cols), BF16), jax.ShapeDtypeStruct((n, SM_COLS), F32)],
        compiler_params=_cparams(("parallel", "arbitrary")),
        name="in_proj",
    )(xn, w, w_small)


POS_SPLIT = 64
POS_TERMS = 3


def _kv_kernel(c_ref, g_ref, wk_ref, wvt_ref, k_ref, vt_ref):
    tk = c_ref.shape[0]
    c = c_ref[...].astype(F32)
    ms = jnp.mean(c * c, axis=-1, keepdims=True)
    cn = (c * lax.rsqrt(ms + NORM_EPS) * g_ref[...]).astype(BF16)
    k = _dot(cn, wk_ref[...]).astype(BF16)
    vt_ref[...] = _dot_nt(wvt_ref[...], cn).astype(BF16)
    pos = pl.program_id(1) * tk + lax.broadcasted_iota(jnp.int32, (tk, A_HEAD_DIM), 0)
    lane = lax.broadcasted_iota(jnp.int32, (tk, A_HEAD_DIM), 1)
    feat = jnp.where(lane < POS_TERMS, pos // POS_SPLIT,
                     jnp.where(lane < 2 * POS_TERMS, pos % POS_SPLIT, 0)).astype(F32).astype(BF16)
    for h in range(A_HEADS):
        k_ref[:, 2 * h * A_HEAD_DIM:(2 * h + 1) * A_HEAD_DIM] = k[:, h * A_HEAD_DIM:(h + 1) * A_HEAD_DIM]
        k_ref[:, (2 * h + 1) * A_HEAD_DIM:(2 * h + 2) * A_HEAD_DIM] = feat


def _kv_expand(p3, g, wk, wvt, tk):
    b, t, _ = p3.shape
    return pl.pallas_call(
        _kv_kernel,
        grid=(b, t // tk),
        in_specs=[
            pl.BlockSpec((None, tk, KV_RANK), lambda i, j: (i, j, OFF_CKV // KV_RANK)),
            pl.BlockSpec((1, KV_RANK), lambda i, j: (0, 0)),
            pl.BlockSpec((KV_RANK, A_WIDTH), lambda i, j: (0, 0)),
            pl.BlockSpec((A_WIDTH, KV_RANK), lambda i, j: (0, 0)),
        ],
        out_specs=[
            pl.BlockSpec((None, tk, 2 * A_WIDTH), lambda i, j: (i, j, 0)),
            pl.BlockSpec((None, None, A_WIDTH, tk), lambda i, j: (i, j, 0, 0)),
        ],
        out_shape=[jax.ShapeDtypeStruct((b, t, 2 * A_WIDTH), BF16),
                   jax.ShapeDtypeStruct((b, t // tk, A_WIDTH, tk), BF16)],
        compiler_params=_cparams(("parallel", "parallel")),
        name="kv_expand",
    )(p3, g, wk, wvt)


LOG2E = 1.4426950408889634
MASK_BIAS = -1e30
M_INIT = -1e29
CNT_ROWS = 32
SEL_UNROLL = 4


def _dsa_kernel(qa_ref, qi_ref, za_ref, smq_ref, smk_ref, gk_ref, k_ref, vt_ref, o_ref,
                kn_ref, sc_ref, qs_ref, m_ref, l_ref, acc_ref, s_ref, *, tq, tk, topk):
    qb = pl.program_id(1)

    @pl.when(qb == 0)
    def _():
        kx = smk_ref[...]
        lane = lax.broadcasted_iota(jnp.int32, (1, SM_COLS), 1)
        is_k = lane < IDX_DIM
        kx = jnp.where(is_k, kx, 0.0)
        ms = jnp.sum(kx * kx, axis=-1, keepdims=True) * (1.0 / IDX_DIM)
        kn = kx * lax.rsqrt(ms + NORM_EPS) * gk_ref[...]
        kn_ref[0] = kn.astype(BF16)
        kn_ref[1] = pltpu.roll(kn, IDX_DIM, 1).astype(BF16)

    nkc = (qb * tq + tq + tk - 1) // tk
    qpos = qb * tq + lax.broadcasted_iota(jnp.int32, (1, tq), 1)
    kpos0 = lax.broadcasted_iota(jnp.int32, (tk, 1), 0)
    w_t = smq_ref[...].T * (IDX_HEADS ** -0.5 * IDX_DIM ** -0.5)

    def score_chunk(c, carry):
        k0 = pl.multiple_of(c * tk, tk)
        acc = jnp.zeros((tk, tq), F32)
        for hp in range(IDX_HEADS // 2):
            qpair = qi_ref[:, hp * LANES:(hp + 1) * LANES]
            for half in range(2):
                row = SM_IDXW + 2 * hp + half
                kc = kn_ref[half, pl.ds(k0, tk), :]
                acc = acc + w_t[row:row + 1, :] * jnp.maximum(_dot_nt(kc, qpair), 0.0)
        bits = pltpu.bitcast(acc, jnp.int32)
        skey = jnp.where(bits < 0, bits ^ jnp.int32(0x7FFFFFFF), bits)
        sc_ref[c] = jnp.where(k0 + kpos0 <= qpos, skey, jnp.int32(INT_MIN))
        return carry

    lax.fori_loop(0, nkc, score_chunk, 0)

    def count_ge(cand):
        def body(c, acc):
            ind = jnp.where(sc_ref[c] >= cand, 1.0, 0.0)
            return acc + jnp.sum(ind.reshape(tk // CNT_ROWS, CNT_ROWS, tq), axis=0)

        acc = lax.fori_loop(0, nkc, body, jnp.zeros((CNT_ROWS, tq), F32))
        return jnp.sum(acc, axis=0, keepdims=True)

    fkeep = jnp.float32(topk)

    def sel_cond(st):
        i, _, done = st
        return (i < 32) & (jnp.min(done) < 0.5)

    def sel_step(st):
        i, prefix, done = st
        for u in range(SEL_UNROLL):
            cand = prefix ^ (jnp.int32(1) << (31 - u - i))
            cnt = count_ge(cand)
            prefix = jnp.where((done < 0.5) & (cnt >= fkeep), cand, prefix)
            done = jnp.where(cnt == fkeep, 1.0, done)
        return i + SEL_UNROLL, prefix, done

    done0 = jnp.where(qpos < topk, 1.0, 0.0)
    _, kth, done = lax.while_loop(
        sel_cond, sel_step, (jnp.int32(0), jnp.full((1, tq), INT_MIN, jnp.int32), done0))
    thr = jnp.maximum(kth, jnp.int32(INT_MIN + 1))

    no_cut = jnp.full((1, tq), np.iinfo(np.int32).max, jnp.int32)

    def tie_cut(_):
        excess = jnp.where(done > 0.5, 0.0, count_ge(thr) - fkeep)

        def count_tied_from(pos):
            def body(c, acc):
                hit = (sc_ref[c] == thr) & (c * tk + kpos0 >= pos)
                ind = jnp.where(hit, 1.0, 0.0)
                return acc + jnp.sum(ind.reshape(tk // CNT_ROWS, CNT_ROWS, tq), axis=0)

            acc = lax.fori_loop(0, nkc, body, jnp.zeros((CNT_ROWS, tq), F32))
            return jnp.sum(acc, axis=0, keepdims=True)

        def pos_step(i, cut):
            cand = cut | (jnp.int32(1) << (pos_bits - 1 - i))
            return jnp.where(count_tied_from(cand) >= excess, cand, cut)

        cut = lax.fori_loop(0, pos_bits, pos_step, jnp.zeros((1, tq), jnp.int32))
        return jnp.where(excess > 0.5, cut, no_cut)

    pos_bits = max(1, (smk_ref.shape[0] - 1).bit_length())
    cut = lax.cond(jnp.min(done) < 0.5, tie_cut, lambda _: no_cut, 0)

    lane = lax.broadcasted_iota(jnp.int32, (1, A_HEAD_DIM), 1)
    for h in range(A_HEADS):
        slope = 2.0 ** (-8.0 * (h + 1) / A_HEADS) * LOG2E
        row = jnp.zeros((1, A_HEAD_DIM), F32)
        rest = np.float32(slope)
        for i in range(POS_TERMS):
            piece = np.float32(rest.astype(BF16))
            rest = np.float32(rest - piece)
            row = jnp.where(lane == i, float(piece) * POS_SPLIT, row)
            row = jnp.where(lane == POS_TERMS + i, float(piece), row)
        hs = slice(h * A_HEAD_DIM, (h + 1) * A_HEAD_DIM)
        qs_ref[:, 2 * h * A_HEAD_DIM:(2 * h + 1) * A_HEAD_DIM] = (
            qa_ref[:, hs].astype(F32) * (A_HEAD_DIM ** -0.5 * LOG2E)).astype(BF16)
        qs_ref[:, (2 * h + 1) * A_HEAD_DIM:(2 * h + 2) * A_HEAD_DIM] = jnp.broadcast_to(
            row, (tq, A_HEAD_DIM)).astype(BF16)
    m_ref[...] = jnp.full(m_ref.shape, M_INIT, F32)
    l_ref[...] = jnp.zeros(l_ref.shape, F32)
    acc_ref[...] = jnp.zeros(acc_ref.shape, F32)

    def qk_scores(c, slot):
        k0 = pl.multiple_of(c * tk, tk)
        for h in range(A_HEADS):
            hs2 = slice(2 * h * A_HEAD_DIM, (2 * h + 2) * A_HEAD_DIM)
            s_ref[slot, h] = _dot_nt(k_ref[pl.ds(k0, tk), hs2], qs_ref[:, hs2])

    def att_chunk(c, carry):
        slot = c % 2
        keys = sc_ref[c]
        keep = (keys > thr) | ((keys == thr) & (c * tk + kpos0 < cut))
        bias = jnp.where(keep, 0.0, MASK_BIAS)
        for h in range(A_HEADS):
            hs = slice(h * A_HEAD_DIM, (h + 1) * A_HEAD_DIM)
            s = s_ref[slot, h] + bias
            m_old = m_ref[h]
            m_new = jnp.maximum(m_old, jnp.max(s, axis=0, keepdims=True))
            p = jnp.exp2(s - m_new)
            alpha = jnp.exp2(m_old - m_new)
            l_ref[h] = alpha * l_ref[h] + jnp.sum(p, axis=0, keepdims=True)
            acc_ref[hs, :] = alpha * acc_ref[hs, :] + _dot(vt_ref[c, hs, :], p.astype(BF16))
            m_ref[h] = m_new
        qk_scores(jnp.minimum(c + 1, nkc - 1), 1 - slot)
        return carry

    qk_scores(0, 0)
    lax.fori_loop(0, nkc, att_chunk, 0)
    for h in range(A_HEADS):
        hs = slice(h * A_HEAD_DIM, (h + 1) * A_HEAD_DIM)
        z = za_ref[:, hs].astype(F32)
        o_ref[:, hs] = ((acc_ref[hs, :] / l_ref[h]).T * _silu(z)).astype(o_ref.dtype)


def _dsa(p3, sm3, kmat, vmat_t, gk, tq, tk, topk):
    b, t, _ = p3.shape
    kern = functools.partial(_dsa_kernel, tq=tq, tk=tk, topk=topk)
    return pl.pallas_call(
        kern,
        grid=(b, t // tq),
        in_specs=[
            pl.BlockSpec((None, tq, A_WIDTH), lambda i, j: (i, j, OFF_QA // A_WIDTH)),
            pl.BlockSpec((None, tq, A_WIDTH), lambda i, j: (i, j, OFF_QIDX // A_WIDTH)),
            pl.BlockSpec((None, tq, A_WIDTH), lambda i, j: (i, j, OFF_ZA // A_WIDTH)),
            pl.BlockSpec((None, tq, SM_COLS), lambda i, j: (i, j, 0)),
            pl.BlockSpec((None, t, SM_COLS), lambda i, j: (i, 0, 0)),
            pl.BlockSpec((1, SM_COLS), lambda i, j: (0, 0)),
            pl.BlockSpec((None, t, 2 * A_WIDTH), lambda i, j: (i, 0, 0)),
            pl.BlockSpec((None, t // tk, A_WIDTH, tk), lambda i, j: (i, 0, 0, 0)),
        ],
        out_specs=pl.BlockSpec((None, tq, A_WIDTH), lambda i, j: (i, j, 0)),
        out_shape=jax.ShapeDtypeStruct((b, t, A_WIDTH), BF16),
        scratch_shapes=[
            pltpu.VMEM((2, t, LANES), BF16),
            pltpu.VMEM((t // tk, tk, tq), jnp.int32),
            pltpu.VMEM((tq, 2 * A_WIDTH), BF16),
            pltpu.VMEM((A_HEADS, 1, tq), F32),
            pltpu.VMEM((A_HEADS, 1, tq), F32),
            pltpu.VMEM((A_WIDTH, tq), F32),
            pltpu.VMEM((2, A_HEADS, tk, tq), F32),
        ],
        compiler_params=_cparams(("parallel", "arbitrary")),
        name="dsa_attention",
    )(p3, p3, p3, sm3, sm3, gk, kmat, vmat_t)


CONF_HALO = 32
CONF_ROWS = 64
CONF_WIN = CONF_ROWS + CONF_HALO


CONF_BLK = 128


def _conf_kernel(xn_ref, w_ref, cw_ref, cb_ref, lg_ref, lb_ref, o_ref, y_ref, hp_ref, c_ref, z_ref,
                 *, seq):
    tm = o_ref.shape[0]
    nb = tm // CONF_BLK
    first = CONF_HALO - (CONV_K - 1)

    def project(r, slot):
        r0 = pl.multiple_of(r * CONF_BLK, CONF_BLK)
        y_ref[slot] = _dot(xn_ref[pl.ds(r0, CONF_BLK), :], w_ref[...])

    def epilogue(r, slot):
        r0 = pl.multiple_of(r * CONF_BLK, CONF_BLK)
        glu = y_ref[slot, :, 0:B_WIDTH] * _sigmoid(y_ref[slot, :, B_WIDTH:2 * B_WIDTH])
        z_ref[...] = _silu(y_ref[slot, :, 2 * B_WIDTH:3 * B_WIDTH])
        history = hp_ref[CONF_BLK:CONF_BLK + CONF_HALO, :]
        hp_ref[0:CONF_HALO, :] = jnp.where(r % (seq // CONF_BLK) == 0, 0.0, history)
        hp_ref[CONF_HALO:CONF_HALO + CONF_BLK, :] = glu
        for ct in range(B_WIDTH // LANES):
            cs = slice(ct * LANES, (ct + 1) * LANES)
            for base in range(0, CONF_BLK, CONF_ROWS):
                win = hp_ref[base:base + CONF_WIN, cs]
                acc = jnp.zeros((CONF_ROWS, LANES), F32)
                for rem in range(SUBLANES):
                    rot = win if rem == 0 else pltpu.roll(win, CONF_WIN - rem, 0)
                    for j in range(CONV_K):
                        if (first + j) % SUBLANES == rem:
                            a = (first + j) - rem
                            acc = acc + cw_ref[j:j + 1, cs] * rot[a:a + CONF_ROWS]
                c_ref[base:base + CONF_ROWS, cs] = acc + cb_ref[:, cs]
        y = c_ref[...]
        mu = jnp.mean(y, axis=-1, keepdims=True)
        yc = y - mu
        var = jnp.mean(yc * yc, axis=-1, keepdims=True)
        yn = yc * lax.rsqrt(var + LN_EPS) * lg_ref[...] + lb_ref[...]
        o_ref[pl.ds(r0, CONF_BLK), :] = (_silu(yn) * z_ref[...]).astype(o_ref.dtype)

    def step(r, carry):
        slot = r % 2
        epilogue(r, slot)
        project(jnp.minimum(r + 1, nb - 1), 1 - slot)
        return carry

    hp_ref[...] = jnp.zeros(hp_ref.shape, F32)
    project(0, 0)
    lax.fori_loop(0, nb, step, 0)


def _conformer(xn, w, cw, cb, lg, lb, tm, seq):
    n, d = xn.shape
    kern = functools.partial(_conf_kernel, seq=seq)
    once = dict(pipeline_mode=pl.Buffered(1))
    vec = lambda r: pl.BlockSpec((r, B_WIDTH), lambda i: (0, 0), **once)
    return pl.pallas_call(
        kern,
        grid=(n // tm,),
        in_specs=[pl.BlockSpec((tm, d), lambda i: (i, 0)),
                  pl.BlockSpec((d, 3 * B_WIDTH), lambda i: (0, 0), **once),
                  vec(CONF_HALO), vec(1), vec(1), vec(1)],
        out_specs=pl.BlockSpec((tm, B_WIDTH), lambda i: (i, 0)),
        out_shape=jax.ShapeDtypeStruct((n, B_WIDTH), BF16),
        scratch_shapes=[pltpu.VMEM((2, CONF_BLK, 3 * B_WIDTH), F32),
                        pltpu.VMEM((CONF_HALO + CONF_BLK, B_WIDTH), F32),
                        pltpu.VMEM((CONF_BLK, B_WIDTH), F32),
                        pltpu.VMEM((CONF_BLK, B_WIDTH), F32)],
        compiler_params=_cparams(("parallel",)),
        name="conformer_proj",
    )(xn, w, cw, cb, lg, lb)


GDN_HALO = 8
QKV_ROWS = 128


def _qkv_kernel(xn_ref, w_ref, cw_ref, o_ref, *, seq):
    j = pl.program_id(1)
    tm, tn = o_ref.shape
    first = GDN_HALO - (SHORT_K - 1)
    tiles_per_part = C_WIDTH // tn
    is_v = j >= 2 * tiles_per_part
    qscale = jnp.where(j < tiles_per_part, C_HEAD_DIM ** -0.5, 1.0)
    rb = min(QKV_ROWS, seq)
    for s0 in range(0, tm, seq):
        tail = jnp.zeros((GDN_HALO, tn), F32)
        for r0 in range(s0, s0 + seq, rb):
            y = _dot(xn_ref[r0:r0 + rb, :], w_ref[...])
            xp = jnp.concatenate([tail, y], axis=0)
            tail = y[rb - GDN_HALO:, :]
            for ct in range(tn // LANES):
                cs = slice(ct * LANES, (ct + 1) * LANES)
                win = xp[:, cs]
                acc = cw_ref[SHORT_K - 1:SHORT_K, cs] * win[GDN_HALO:]
                for back in range(1, SHORT_K):
                    tap = SHORT_K - 1 - back
                    acc = acc + cw_ref[tap:tap + 1, cs] * pltpu.roll(win, back, 0)[GDN_HALO:]
                a = _silu(acc)
                rs = lax.rsqrt(jnp.sum(a * a, axis=-1, keepdims=True) + L2_EPS)
                o_ref[r0:r0 + rb, cs] = (a * (jnp.where(is_v, 1.0, rs) * qscale)).astype(BF16)


def _qkv_proj(xn, w, cw, tm, tn, seq):
    n, d = xn.shape
    cols = w.shape[1]
    kern = functools.partial(_qkv_kernel, seq=seq)
    return pl.pallas_call(
        kern,
        grid=(n // tm, cols // tn),
        in_specs=[
            pl.BlockSpec((tm, d), lambda i, j: (i, 0)),
            pl.BlockSpec((d, tn), lambda i, j: (0, j)),
            pl.BlockSpec((SUBLANES, tn), lambda i, j: (0, j)),
        ],
        out_specs=pl.BlockSpec((tm, tn), lambda i, j: (i, j)),
        out_shape=jax.ShapeDtypeStruct((n, cols), BF16),
        compiler_params=_cparams(("parallel", "arbitrary")),
        name="qkv_proj",
    )(xn, w, cw)


def _gdn_gate_kernel(sm_ref, a_ref, dt_ref, o_ref):
    t = sm_ref.shape[0]
    x = sm_ref[...]
    lane = lax.broadcasted_iota(jnp.int32, (1, SM_COLS), 1)
    beta = _sigmoid(x)
    y = x + dt_ref[...]
    softplus = jnp.maximum(y, 0.0) + jnp.log1p(jnp.exp(-jnp.abs(y)))
    g = -jnp.exp(a_ref[...]) * softplus
    row = lax.broadcasted_iota(jnp.int32, (CHUNK, CHUNK), 0)
    col = lax.broadcasted_iota(jnp.int32, (CHUNK, CHUNK), 1)
    tril = (row >= col).astype(F32)
    is_beta = (lane >= SM_BETA) & (lane < SM_BETA + C_HEADS)
    for c in range(t // CHUNK):
        rs = slice(c * CHUNK, (c + 1) * CHUNK)
        gc = jnp.dot(tril, g[rs], preferred_element_type=F32, precision=lax.Precision.HIGHEST)
        o_ref[rs, :] = jnp.where(is_beta, beta[rs], gc)


def _gdn_gate(sm3, a_vec, dt_vec):
    b, t, _ = sm3.shape
    return pl.pallas_call(
        _gdn_gate_kernel,
        grid=(b,),
        in_specs=[
            pl.BlockSpec((None, t, SM_COLS), lambda i: (i, 0, 0)),
            pl.BlockSpec((1, SM_COLS), lambda i: (0, 0)),
            pl.BlockSpec((1, SM_COLS), lambda i: (0, 0)),
        ],
        out_specs=pl.BlockSpec((None, t, SM_COLS), lambda i: (i, 0, 0)),
        out_shape=jax.ShapeDtypeStruct((b, t, SM_COLS), F32),
        compiler_params=_cparams(("parallel",)),
        name="gdn_gate",
    )(sm3, a_vec, dt_vec)


GDN_BATCH = 32


def _unit_lower_inverse(m, idx_r, idx_c):
    n = m.shape[-1]
    eye = (idx_r == idx_c).astype(F32)
    blk = lambda s: (idx_r // s) == (idx_c // s)
    m0 = jnp.where(blk(8), m, 0.0)
    m0b = m0.astype(BF16)
    a2 = _bdot(m0b, m0b)
    p = eye - m0
    p = p + _bdot(p.astype(BF16), a2.astype(BF16))
    a4 = _bdot(a2.astype(BF16), a2.astype(BF16))
    p = p + _bdot(p.astype(BF16), a4.astype(BF16))
    s = 8
    while s < n:
        lo = jnp.where(blk(2 * s) & jnp.logical_not(blk(s)), m, 0.0)
        pb = p.astype(BF16)
        p = p - _bdot(pb, _bdot(lo.astype(BF16), pb).astype(BF16))
        s *= 2
    return p


def _gdn_kernel(q_ref, k_ref, v_ref, z_ref, bcol_ref, gcol_ref, grow_ref, on_ref, o_ref,
                a_ref, b_ref, qp_ref, op_ref, s_ref, eg_ref, st_ref, *, hb, hg):
    tb = q_ref.shape[0]
    nc = tb // CHUNK
    idx_r = lax.broadcasted_iota(jnp.int32, (1, CHUNK, CHUNK), 1)
    idx_c = lax.broadcasted_iota(jnp.int32, (1, CHUNK, CHUNK), 2)
    causal = idx_r >= idx_c
    strict = idx_r > idx_c

    @pl.when(pl.program_id(2) == 0)
    def _():
        st_ref[...] = jnp.zeros(st_ref.shape, F32)

    heads = lambda ref: lambda h: ref[:, h * C_HEAD_DIM:(h + 1) * C_HEAD_DIM].astype(F32).reshape(
        nc, CHUNK, C_HEAD_DIM)
    cols = lambda ref: lambda h: ref[:, h:h + 1].reshape(nc, CHUNK, 1)
    for h0 in range(0, hb, hg):
        stack = lambda f: jnp.concatenate([f(h) for h in range(h0, h0 + hg)], axis=0)
        hsl = slice(h0, h0 + hg)
        q = stack(heads(q_ref))
        k = stack(heads(k_ref))
        v = stack(heads(v_ref))
        beta = stack(cols(bcol_ref))
        gc = stack(cols(gcol_ref))
        gr = stack(lambda h: grow_ref[h])
        decay = jnp.exp(jnp.where(causal, gc - gr, -jnp.inf))
        kb = k * beta
        kbf = k.astype(BF16)
        m = jnp.where(strict, _bdot_nt(kb.astype(BF16), kbf) * decay, 0.0)
        tinv = _unit_lower_inverse(m, idx_r, idx_c)
        rhs = jnp.concatenate([v * beta, kb * jnp.exp(gc)], axis=-1)
        sol = _bdot(tinv.astype(BF16), rhs.astype(BF16)).astype(BF16)
        ub = sol[:, :, :C_HEAD_DIM]
        wb = sol[:, :, C_HEAD_DIM:]
        glast = gc[:, CHUNK - 1:CHUNK, :]
        kd = (k * jnp.exp(glast - gc)).astype(BF16)
        aqk = (_bdot_nt(q.astype(BF16), kbf) * decay).astype(BF16)
        unstack = lambda a: a.reshape((hg, nc) + a.shape[1:])
        a_ref[hsl] = unstack((-_bdot_tn(kd, wb)).astype(BF16))
        b_ref[hsl] = unstack(_bdot_tn(kd, ub).astype(BF16))
        qp_ref[hsl] = unstack((q * jnp.exp(gc) - _bdot(aqk, wb)).astype(BF16))
        op_ref[hsl] = unstack(_bdot(aqk, ub))
        eg_ref[hsl] = unstack(jnp.broadcast_to(jnp.exp(glast), (hg * nc, 1, C_HEAD_DIM)))

    def step(c, carry):
        for h in range(hb):
            s = st_ref[h]
            sb = s.astype(BF16)
            s_ref[h, c] = sb
            st_ref[h] = s * eg_ref[h, c] + _dot(a_ref[h, c], sb) + b_ref[h, c].astype(F32)
        return carry

    lax.fori_loop(0, nc, step, 0)

    for h in range(hb):
        hs = slice(h * C_HEAD_DIM, (h + 1) * C_HEAD_DIM)
        o = (_bdot(qp_ref[h], s_ref[h]) + op_ref[h]).reshape(tb, C_HEAD_DIM)
        ms = jnp.mean(o * o, axis=-1, keepdims=True)
        on = o * lax.rsqrt(ms + NORM_EPS) * on_ref[...]
        z = z_ref[:, hs].astype(F32)
        o_ref[:, hs] = (on * _silu(z)).astype(o_ref.dtype)


def _gdn(qkv3, p3, bcol, gcol, grow, on_g, hb, tb):
    b, t, _ = qkv3.shape
    nc = tb // CHUNK
    wd = hb * C_HEAD_DIM
    hg = min(hb, max(1, GDN_BATCH // nc))
    kern = functools.partial(_gdn_kernel, hb=hb, hg=hg)
    part = lambda which: pl.BlockSpec(
        (None, tb, wd), lambda i, j, s: (i, s, which * (C_WIDTH // wd) + j))
    return pl.pallas_call(
        kern,
        grid=(b, C_HEADS // hb, t // tb),
        in_specs=[
            part(0), part(1), part(2),
            pl.BlockSpec((None, tb, wd), lambda i, j, s: (i, s, OFF_ZC // wd + j)),
            pl.BlockSpec((None, None, tb, hb), lambda i, j, s: (i, j, s, 0)),
            pl.BlockSpec((None, None, tb, hb), lambda i, j, s: (i, j, s, 0)),
            pl.BlockSpec((None, hb, nc, 1, CHUNK), lambda i, j, s: (i, j, s, 0, 0)),
            pl.BlockSpec((1, C_HEAD_DIM), lambda i, j, s: (0, 0)),
        ],
        out_specs=pl.BlockSpec((None, tb, wd), lambda i, j, s: (i, s, j)),
        out_shape=jax.ShapeDtypeStruct((b, t, C_WIDTH), BF16),
        scratch_shapes=[
            pltpu.VMEM((hb, nc, C_HEAD_DIM, C_HEAD_DIM), BF16),
            pltpu.VMEM((hb, nc, C_HEAD_DIM, C_HEAD_DIM), BF16),
            pltpu.VMEM((hb, nc, CHUNK, C_HEAD_DIM), BF16),
            pltpu.VMEM((hb, nc, CHUNK, C_HEAD_DIM), F32),
            pltpu.VMEM((hb, nc, C_HEAD_DIM, C_HEAD_DIM), BF16),
            pltpu.VMEM((hb, nc, 1, C_HEAD_DIM), F32),
            pltpu.VMEM((hb, C_HEAD_DIM, C_HEAD_DIM), F32),
        ],
        compiler_params=_cparams(("parallel", "parallel", "arbitrary")),
        name="gdn_delta_rule",
    )(qkv3, qkv3, qkv3, p3, bcol, gcol, grow, on_g)


def _merge_kernel(ua_ref, ub_ref, uc_ref, g0_ref, g1_ref, g2_ref, wa_ref, wb_ref, wc_ref, o_ref):
    ya = _dot(ua_ref[...], wa_ref[...])
    yb = _dot(ub_ref[...], wb_ref[...])
    yc = _dot(uc_ref[...], wc_ref[...])
    o = (_sigmoid(g0_ref[...].astype(F32)) * ya + _sigmoid(g1_ref[...].astype(F32)) * yb
         + _sigmoid(g2_ref[...].astype(F32)) * yc)
    o_ref[...] = o.astype(o_ref.dtype)


def _merge(ua, ub, uc, p, wa, wb, wc, tm, tn):
    n = ua.shape[0]
    gate = lambda br: pl.BlockSpec((tm, tn), lambda i, j: (i, (OFF_GATE + br * D_MODEL) // tn + j))
    return pl.pallas_call(
        _merge_kernel,
        grid=(n // tm, D_MODEL // tn),
        in_specs=[
            pl.BlockSpec((tm, A_WIDTH), lambda i, j: (i, 0)),
            pl.BlockSpec((tm, B_WIDTH), lambda i, j: (i, 0)),
            pl.BlockSpec((tm, C_WIDTH), lambda i, j: (i, 0)),
            gate(0), gate(1), gate(2),
            pl.BlockSpec((A_WIDTH, tn), lambda i, j: (0, j)),
            pl.BlockSpec((B_WIDTH, tn), lambda i, j: (0, j)),
            pl.BlockSpec((C_WIDTH, tn), lambda i, j: (0, j)),
        ],
        out_specs=pl.BlockSpec((tm, tn), lambda i, j: (i, j)),
        out_shape=jax.ShapeDtypeStruct((n, D_MODEL), BF16),
        compiler_params=_cparams(("parallel", "arbitrary")),
        name="gate_merge",
    )(ua, ub, uc, p, p, p, wa, wb, wc)


def _out_kernel(x_ref, m_ref, w_ref, g_ref, *o_refs, final):
    y = x_ref[...] + _dot(m_ref[...], w_ref[...])
    if final:
        o_refs[0][...] = _rmsnorm_rows(y, g_ref[...])
    else:
        o_refs[0][...] = y
        o_refs[1][...] = _rmsnorm_rows(y, g_ref[...]).astype(BF16)


def _out_proj(x2d, merged, w, g, tm, final):
    n, d = x2d.shape
    kern = functools.partial(_out_kernel, final=final)
    row = pl.BlockSpec((tm, d), lambda i: (i, 0))
    return pl.pallas_call(
        kern,
        grid=(n // tm,),
        in_specs=[row, row, pl.BlockSpec((d, d), lambda i: (0, 0)), pl.BlockSpec((1, d), lambda i: (0, 0))],
        out_specs=row if final else [row, row],
        out_shape=(jax.ShapeDtypeStruct((n, d), F32) if final else
                   [jax.ShapeDtypeStruct((n, d), F32), jax.ShapeDtypeStruct((n, d), BF16)]),
        compiler_params=_cparams(("parallel",)),
        name="out_proj",
    )(x2d, merged, w, g)


_SPLITS = (A_WIDTH, KV_RANK, IDX_HEADS * IDX_DIM, IDX_DIM, IDX_HEADS, A_WIDTH, B_WIDTH, B_WIDTH,
           B_WIDTH, 3 * C_WIDTH, C_HEADS, C_HEADS, C_WIDTH, N_BRANCH * D_MODEL)


def _repack_w_in(w):
    offs = np.concatenate([[0], np.cumsum(_SPLITS)])
    seg = [w[:, offs[i]:offs[i + 1]] for i in range(len(_SPLITS))]
    (q_a, c_kv, q_idx, k_idx, idx_w, z_a, glu_v, glu_g, z_b, qkv, beta, alpha, z_c, gates) = seg
    big = jnp.concatenate([q_a, q_idx, z_a, z_c, gates, c_kv], axis=1)
    pad = jnp.zeros((w.shape[0], SM_COLS - (SM_ALPHA + C_HEADS)), w.dtype)
    small = jnp.concatenate([k_idx, idx_w, beta, alpha, pad], axis=1)
    glu = jnp.concatenate([glu_v, glu_g, z_b], axis=1)
    return big.astype(BF16), small.astype(BF16), qkv.astype(BF16), glu.astype(BF16)


def _pad_lanes(vec, off):
    out = jnp.zeros((1, SM_COLS), F32)
    return out.at[0, off:off + vec.shape[0]].set(vec.astype(F32))


def _pick(total, pref):
    for c in pref:
        if total % c == 0:
            return c
    return total


def _tiles(n, t):
    return dict(tm_proj=t, tq=_pick(t, (256, 128)),
                tk=_pick(t, (256, 128)), tt=_pick(t, (512, 256, 128)),
                tm=_pick(n, (512, 256)), tm_merge=_pick(n, (1024, 512, 256)), hb=8,
                tb=_pick(t, (512, 256, 128)))


def kernel(x, norm_g, w_in, kv_norm_g, idx_k_norm_g, w_uk, w_uv, w_proj_a, conv_w_b, conv_b_b,
           ln_g_b, ln_b_b, w_proj_b, conv_w_c, a_log, dt_bias, onorm_g_c, w_proj_c, w_out, final_g):
    b, t, _ = x.shape
    return _forward(_tiles(b * t, t), x, norm_g, w_in, kv_norm_g, idx_k_norm_g, w_uk, w_uv, w_proj_a,
                    conv_w_b, conv_b_b, ln_g_b, ln_b_b, w_proj_b, conv_w_c, a_log, dt_bias,
                    onorm_g_c, w_proj_c, w_out, final_g)


def _forward(tiles, x, norm_g, w_in, kv_norm_g, idx_k_norm_g, w_uk, w_uv, w_proj_a, conv_w_b,
             conv_b_b, ln_g_b, ln_b_b, w_proj_b, conv_w_c, a_log, dt_bias, onorm_g_c, w_proj_c,
             w_out, final_g):
    b, t, d = x.shape
    n = b * t
    depth = norm_g.shape[0]
    topk = min(TOPK_MAX, t // 4)
    nc = t // CHUNK
    tm_proj, tq, tk, tt, tm, tm_merge, hb, tb = (
        tiles[k] for k in ("tm_proj", "tq", "tk", "tt", "tm", "tm_merge", "hb", "tb"))

    x2d = x.reshape(n, d)
    xn = _rmsnorm(x2d, norm_g[0][None, :].astype(F32), tm)
    for l in range(depth):
        w_big, w_small, w_qkv, w_glu = _repack_w_in(w_in[l])
        assert tm_proj % t == 0, "a projection row tile must hold whole sequences"
        p, sm = _proj(xn, w_big, w_small, tm_proj, 512)
        p3 = p.reshape(b, t, P_COLS)
        sm3 = sm.reshape(b, t, SM_COLS)

        kmat, vmat_t = _kv_expand(p3, kv_norm_g[l][None, :].astype(F32),
                                  w_uk[l].reshape(KV_RANK, A_WIDTH).astype(BF16),
                                  w_uv[l].reshape(KV_RANK, A_WIDTH).T.astype(BF16), tk)
        u_a = _dsa(p3, sm3, kmat, vmat_t, _pad_lanes(idx_k_norm_g[l], SM_KIDX), tq, tk, topk)

        cw_b = jnp.concatenate(
            [conv_w_b[l].astype(F32), jnp.zeros((CONF_HALO - CONV_K, B_WIDTH), F32)], axis=0)
        u_b = _conformer(xn, w_glu, cw_b, conv_b_b[l][None, :].astype(F32),
                         ln_g_b[l][None, :].astype(F32), ln_b_b[l][None, :].astype(F32), tm_proj, t)

        cw_c = jnp.concatenate(
            [conv_w_c[l].astype(F32), jnp.zeros((SUBLANES - SHORT_K, 3 * C_WIDTH), F32)], axis=0)
        qkv3 =_qkv_proj(xn, w_qkv, cw_c, tm_proj, 512, t).reshape(b, t, 3 * C_WIDTH)
        gate = _gdn_gate(sm3, _pad_lanes(a_log[l], SM_ALPHA), _pad_lanes(dt_bias[l], SM_ALPHA))
        beta = jnp.transpose(gate[:, :, SM_BETA:SM_BETA + C_HEADS], (0, 2, 1))
        gcum = jnp.transpose(gate[:, :, SM_ALPHA:SM_ALPHA + C_HEADS], (0, 2, 1))
        col = lambda a: jnp.transpose(a.reshape(b, C_HEADS // hb, hb, t), (0, 1, 3, 2))
        u_c = _gdn(qkv3, p3, col(beta), col(gcum),
                   gcum.reshape(b, C_HEADS, nc, 1, CHUNK), onorm_g_c[l][None, :].astype(F32), hb, tb)

        merged = _merge(u_a.reshape(n, A_WIDTH), u_b.reshape(n, B_WIDTH), u_c.reshape(n, C_WIDTH), p,
                        w_proj_a[l].astype(BF16), w_proj_b[l].astype(BF16), w_proj_c[l].astype(BF16),
                        tm_merge, 512)
        if l == depth - 1:
            x2d = _out_proj(x2d, merged, w_out[l].astype(BF16), final_g[None, :].astype(F32), tm, True)
        else:
            x2d, xn = _out_proj(x2d, merged, w_out[l].astype(BF16),
                                norm_g[l + 1][None, :].astype(F32), tm, False)
    return x2d.reshape(b, t, d)
```

```python
import functools
import math

import jax
import jax.numpy as jnp
import numpy as np
from jax import lax
from jax.experimental import pallas as pl
from jax.experimental.pallas import tpu as pltpu

F32 = jnp.float32
BF16 = jnp.bfloat16

D_MODEL = 2048
NORM_EPS = 1e-6
LN_EPS = 1e-5
L2_EPS = 1e-6
A_HEADS = 8
A_HEAD_DIM = 128
A_WIDTH = A_HEADS * A_HEAD_DIM
KV_RANK = 512
IDX_HEADS = 16
IDX_DIM = 64
TOPK_MAX = 256
B_WIDTH = 1024
CONV_K = 31
C_HEADS = 16
C_HEAD_DIM = 128
C_WIDTH = C_HEADS * C_HEAD_DIM
SHORT_K = 4
CHUNK = 64
N_BRANCH = 3

LANES = 128
SUBLANES = 8
VMEM_LIMIT = 56 * 1024 * 1024

OFF_QA = 0
OFF_QIDX = OFF_QA + A_WIDTH
OFF_ZA = OFF_QIDX + IDX_HEADS * IDX_DIM
OFF_ZC = OFF_ZA + A_WIDTH
OFF_GATE = OFF_ZC + C_WIDTH
OFF_CKV = OFF_GATE + N_BRANCH * D_MODEL
P_COLS = OFF_CKV + KV_RANK
SM_KIDX = 0
SM_IDXW = SM_KIDX + IDX_DIM
SM_BETA = SM_IDXW + IDX_HEADS
SM_ALPHA = SM_BETA + C_HEADS
SM_COLS = LANES

INT_MIN = -(2**31)


def _cparams(sem):
    return pltpu.CompilerParams(dimension_semantics=sem, vmem_limit_bytes=VMEM_LIMIT)


def _sigmoid(x):
    return 1.0 / (1.0 + jnp.exp(-x))


def _silu(x):
    return x * _sigmoid(x)


def _dot(a, b):
    return jnp.dot(a, b, preferred_element_type=F32)


def _dot_nt(a, b):
    return lax.dot_general(a, b, (((1,), (1,)), ((), ())), preferred_element_type=F32)


def _bdot(a, b):
    return lax.dot_general(a, b, (((2,), (1,)), ((0,), (0,))), preferred_element_type=F32)


def _bdot_tn(a, b):
    return lax.dot_general(a, b, (((1,), (1,)), ((0,), (0,))), preferred_element_type=F32)


def _bdot_nt(a, b):
    return lax.dot_general(a, b, (((2,), (2,)), ((0,), (0,))), preferred_element_type=F32)


def _rmsnorm_rows(x, g):
    ms = jnp.mean(x * x, axis=-1, keepdims=True)
    return x * lax.rsqrt(ms + NORM_EPS) * g


def _rms_kernel(x_ref, g_ref, o_ref):
    o_ref[...] = _rmsnorm_rows(x_ref[...], g_ref[...]).astype(o_ref.dtype)


def _rmsnorm(x2d, g, tm):
    n, d = x2d.shape
    return pl.pallas_call(
        _rms_kernel,
        grid=(n // tm,),
        in_specs=[pl.BlockSpec((tm, d), lambda i: (i, 0)), pl.BlockSpec((1, d), lambda i: (0, 0))],
        out_specs=pl.BlockSpec((tm, d), lambda i: (i, 0)),
        out_shape=jax.ShapeDtypeStruct((n, d), BF16),
        compiler_params=_cparams(("parallel",)),
        name="rmsnorm",
    )(x2d, g)


def _proj_kernel(xn_ref, w_ref, ws_ref, o_ref, os_ref):
    xn = xn_ref[...]
    o_ref[...] = _dot(xn, w_ref[...]).astype(o_ref.dtype)

    @pl.when(pl.program_id(1) == 0)
    def _():
        os_ref[...] = _dot(xn, ws_ref[...])


def _proj(xn, w, w_small, tm, tn):
    n, d = xn.shape
    cols = w.shape[1]
    return pl.pallas_call(
        _proj_kernel,
        grid=(n // tm, cols // tn),
        in_specs=[
            pl.BlockSpec((tm, d), lambda i, j: (i, 0)),
            pl.BlockSpec((d, tn), lambda i, j: (0, j)),
            pl.BlockSpec((d, SM_COLS), lambda i, j: (0, 0)),
        ],
        out_specs=[pl.BlockSpec((tm, tn), lambda i, j: (i, j)),
                   pl.BlockSpec((tm, SM_COLS), lambda i, j: (i, 0))],
        out_shape=[jax.ShapeDtypeStruct((n, cols), BF16), jax.ShapeDtypeStruct((n, SM_COLS), F32)],
        compiler_params=_cparams(("parallel", "arbitrary")),
        name="in_proj",
    )(xn, w, w_small)


POS_SPLIT = 64
POS_TERMS = 3


def _kv_kernel(c_ref, g_ref, wk_ref, wvt_ref, k_ref, vt_ref):
    tk = c_ref.shape[0]
    c = c_ref[...].astype(F32)
    ms = jnp.mean(c * c, axis=-1, keepdims=True)
    cn = (c * lax.rsqrt(ms + NORM_EPS) * g_ref[...]).astype(BF16)
    k = _dot(cn, wk_ref[...]).astype(BF16)
    vt_ref[...] = _dot_nt(wvt_ref[...], cn).astype(BF16)
    pos = pl.program_id(1) * tk + lax.broadcasted_iota(jnp.int32, (tk, A_HEAD_DIM), 0)
    lane = lax.broadcasted_iota(jnp.int32, (tk, A_HEAD_DIM), 1)
    feat = jnp.where(lane < POS_TERMS, pos // POS_SPLIT,
                     jnp.where(lane < 2 * POS_TERMS, pos % POS_SPLIT, 0)).astype(F32).astype(BF16)
    for h in range(A_HEADS):
        k_ref[:, 2 * h * A_HEAD_DIM:(2 * h + 1) * A_HEAD_DIM] = k[:, h * A_HEAD_DIM:(h + 1) * A_HEAD_DIM]
        k_ref[:, (2 * h + 1) * A_HEAD_DIM:(2 * h + 2) * A_HEAD_DIM] = feat


def _kv_expand(p3, g, wk, wvt, tk):
    b, t, _ = p3.shape
    return pl.pallas_call(
        _kv_kernel,
        grid=(b, t // tk),
        in_specs=[
            pl.BlockSpec((None, tk, KV_RANK), lambda i, j: (i, j, OFF_CKV // KV_RANK)),
            pl.BlockSpec((1, KV_RANK), lambda i, j: (0, 0)),
            pl.BlockSpec((KV_RANK, A_WIDTH), lambda i, j: (0, 0)),
            pl.BlockSpec((A_WIDTH, KV_RANK), lambda i, j: (0, 0)),
        ],
        out_specs=[
            pl.BlockSpec((None, tk, 2 * A_WIDTH), lambda i, j: (i, j, 0)),
            pl.BlockSpec((None, None, A_WIDTH, tk), lambda i, j: (i, j, 0, 0)),
        ],
        out_shape=[jax.ShapeDtypeStruct((b, t, 2 * A_WIDTH), BF16),
                   jax.ShapeDtypeStruct((b, t // tk, A_WIDTH, tk), BF16)],
        compiler_params=_cparams(("parallel", "parallel")),
        name="kv_expand",
    )(p3, g, wk, wvt)


LOG2E = 1.4426950408889634
MASK_BIAS = -1e30
M_INIT = -1e29
CNT_ROWS = 32
SEL_UNROLL = 4


def _dsa_kernel(qa_ref, qi_ref, za_ref, smq_ref, smk_ref, gk_ref, k_ref, vt_ref, o_ref,
                kn_ref, sc_ref, qs_ref, m_ref, l_ref, acc_ref, s_ref, *, tq, tk, topk):
    qb = pl.program_id(1)

    @pl.when(qb == 0)
    def _():
        kx = smk_ref[...]
        lane = lax.broadcasted_iota(jnp.int32, (1, SM_COLS), 1)
        is_k = lane < IDX_DIM
        kx = jnp.where(is_k, kx, 0.0)
        ms = jnp.sum(kx * kx, axis=-1, keepdims=True) * (1.0 / IDX_DIM)
        kn = kx * lax.rsqrt(ms + NORM_EPS) * gk_ref[...]
        kn_ref[0] = kn.astype(BF16)
        kn_ref[1] = pltpu.roll(kn, IDX_DIM, 1).astype(BF16)

    nkc = (qb * tq + tq + tk - 1) // tk
    qpos = qb * tq + lax.broadcasted_iota(jnp.int32, (1, tq), 1)
    kpos0 = lax.broadcasted_iota(jnp.int32, (tk, 1), 0)
    w_t = smq_ref[...].T * (IDX_HEADS ** -0.5 * IDX_DIM ** -0.5)

    def score_chunk(c, carry):
        k0 = pl.multiple_of(c * tk, tk)
        acc = jnp.zeros((tk, tq), F32)
        for hp in range(IDX_HEADS // 2):
            qpair = qi_ref[:, hp * LANES:(hp + 1) * LANES]
            for half in range(2):
                row = SM_IDXW + 2 * hp + half
                kc = kn_ref[half, pl.ds(k0, tk), :]
                acc = acc + w_t[row:row + 1, :] * jnp.maximum(_dot_nt(kc, qpair), 0.0)
        bits = pltpu.bitcast(acc, jnp.int32)
        skey = jnp.where(bits < 0, bits ^ jnp.int32(0x7FFFFFFF), bits)
        sc_ref[c] = jnp.where(k0 + kpos0 <= qpos, skey, jnp.int32(INT_MIN))
        return carry

    lax.fori_loop(0, nkc, score_chunk, 0)

    def count_ge(cand):
        def body(c, acc):
            ind = jnp.where(sc_ref[c] >= cand, 1.0, 0.0)
            return acc + jnp.sum(ind.reshape(tk // CNT_ROWS, CNT_ROWS, tq), axis=0)

        acc = lax.fori_loop(0, nkc, body, jnp.zeros((CNT_ROWS, tq), F32))
        return jnp.sum(acc, axis=0, keepdims=True)

    fkeep = jnp.float32(topk)

    def sel_cond(st):
        i, _, done = st
        return (i < 32) & (jnp.min(done) < 0.5)

    def sel_step(st):
        i, prefix, done = st
        for u in range(SEL_UNROLL):
            cand = prefix ^ (jnp.int32(1) << (31 - u - i))
            cnt = count_ge(cand)
            prefix = jnp.where((done < 0.5) & (cnt >= fkeep), cand, prefix)
            done = jnp.where(cnt == fkeep, 1.0, done)
        return i + SEL_UNROLL, prefix, done

    done0 = jnp.where(qpos < topk, 1.0, 0.0)
    _, kth, done = lax.while_loop(
        sel_cond, sel_step, (jnp.int32(0), jnp.full((1, tq), INT_MIN, jnp.int32), done0))
    thr = jnp.maximum(kth, jnp.int32(INT_MIN + 1))

    no_cut = jnp.full((1, tq), np.iinfo(np.int32).max, jnp.int32)

    def tie_cut(_):
        excess = jnp.where(done > 0.5, 0.0, count_ge(thr) - fkeep)

        def count_tied_from(pos):
            def body(c, acc):
                hit = (sc_ref[c] == thr) & (c * tk + kpos0 >= pos)
                ind = jnp.where(hit, 1.0, 0.0)
                return acc + jnp.sum(ind.reshape(tk // CNT_ROWS, CNT_ROWS, tq), axis=0)

            acc = lax.fori_loop(0, nkc, body, jnp.zeros((CNT_ROWS, tq), F32))
            return jnp.sum(acc, axis=0, keepdims=True)

        def pos_step(i, cut):
            cand = cut | (jnp.int32(1) << (pos_bits - 1 - i))
            return jnp.where(count_tied_from(cand) >= excess, cand, cut)

        cut = lax.fori_loop(0, pos_bits, pos_step, jnp.zeros((1, tq), jnp.int32))
        return jnp.where(excess > 0.5, cut, no_cut)

    pos_bits = max(1, (smk_ref.shape[0] - 1).bit_length())
    cut = lax.cond(jnp.min(done) < 0.5, tie_cut, lambda _: no_cut, 0)

    lane = lax.broadcasted_iota(jnp.int32, (1, A_HEAD_DIM), 1)
    for h in range(A_HEADS):
        slope = 2.0 ** (-8.0 * (h + 1) / A_HEADS) * LOG2E
        row = jnp.zeros((1, A_HEAD_DIM), F32)
        rest = np.float32(slope)
        for i in range(POS_TERMS):
            piece = np.float32(rest.astype(BF16))
            rest = np.float32(rest - piece)
            row = jnp.where(lane == i, float(piece) * POS_SPLIT, row)
            row = jnp.where(lane == POS_TERMS + i, float(piece), row)
        hs = slice(h * A_HEAD_DIM, (h + 1) * A_HEAD_DIM)
        qs_ref[:, 2 * h * A_HEAD_DIM:(2 * h + 1) * A_HEAD_DIM] = (
            qa_ref[:, hs].astype(F32) * (A_HEAD_DIM ** -0.5 * LOG2E)).astype(BF16)
        qs_ref[:, (2 * h + 1) * A_HEAD_DIM:(2 * h + 2) * A_HEAD_DIM] = jnp.broadcast_to(
            row, (tq, A_HEAD_DIM)).astype(BF16)
    m_ref[...] = jnp.full(m_ref.shape, M_INIT, F32)
    l_ref[...] = jnp.zeros(l_ref.shape, F32)
    acc_ref[...] = jnp.zeros(acc_ref.shape, F32)

    def qk_scores(c, slot):
        k0 = pl.multiple_of(c * tk, tk)
        for h in range(A_HEADS):
            hs2 = slice(2 * h * A_HEAD_DIM, (2 * h + 2) * A_HEAD_DIM)
            s_ref[slot, h] = _dot_nt(k_ref[pl.ds(k0, tk), hs2], qs_ref[:, hs2])

    def att_chunk(c, carry):
        slot = c % 2
        keys = sc_ref[c]
        keep = (keys > thr) | ((keys == thr) & (c * tk + kpos0 < cut))
        bias = jnp.where(keep, 0.0, MASK_BIAS)
        for h in range(A_HEADS):
            hs = slice(h * A_HEAD_DIM, (h + 1) * A_HEAD_DIM)
            s = s_ref[slot, h] + bias
            m_old = m_ref[h]
            m_new = jnp.maximum(m_old, jnp.max(s, axis=0, keepdims=True))
            p = jnp.exp2(s - m_new)
            alpha = jnp.exp2(m_old - m_new)
            l_ref[h] = alpha * l_ref[h] + jnp.sum(p, axis=0, keepdims=True)
            acc_ref[hs, :] = alpha * acc_ref[hs, :] + _dot(vt_ref[c, hs, :], p.astype(BF16))
            m_ref[h] = m_new
        qk_scores(jnp.minimum(c + 1, nkc - 1), 1 - slot)
        return carry

    qk_scores(0, 0)
    lax.fori_loop(0, nkc, att_chunk, 0)
    for h in range(A_HEADS):
        hs = slice(h * A_HEAD_DIM, (h + 1) * A_HEAD_DIM)
        z = za_ref[:, hs].astype(F32)
        o_ref[:, hs] = ((acc_ref[hs, :] / l_ref[h]).T * _silu(z)).astype(o_ref.dtype)


def _dsa(p3, sm3, kmat, vmat_t, gk, tq, tk, topk):
    b, t, _ = p3.shape
    kern = functools.partial(_dsa_kernel, tq=tq, tk=tk, topk=topk)
    return pl.pallas_call(
        kern,
        grid=(b, t // tq),
        in_specs=[
            pl.BlockSpec((None, tq, A_WIDTH), lambda i, j: (i, j, OFF_QA // A_WIDTH)),
            pl.BlockSpec((None, tq, A_WIDTH), lambda i, j: (i, j, OFF_QIDX // A_WIDTH)),
            pl.BlockSpec((None, tq, A_WIDTH), lambda i, j: (i, j, OFF_ZA // A_WIDTH)),
            pl.BlockSpec((None, tq, SM_COLS), lambda i, j: (i, j, 0)),
            pl.BlockSpec((None, t, SM_COLS), lambda i, j: (i, 0, 0)),
            pl.BlockSpec((1, SM_COLS), lambda i, j: (0, 0)),
            pl.BlockSpec((None, t, 2 * A_WIDTH), lambda i, j: (i, 0, 0)),
            pl.BlockSpec((None, t // tk, A_WIDTH, tk), lambda i, j: (i, 0, 0, 0)),
        ],
        out_specs=pl.BlockSpec((None, tq, A_WIDTH), lambda i, j: (i, j, 0)),
        out_shape=jax.ShapeDtypeStruct((b, t, A_WIDTH), BF16),
        scratch_shapes=[
            pltpu.VMEM((2, t, LANES), BF16),
            pltpu.VMEM((t // tk, tk, tq), jnp.int32),
            pltpu.VMEM((tq, 2 * A_WIDTH), BF16),
            pltpu.VMEM((A_HEADS, 1, tq), F32),
            pltpu.VMEM((A_HEADS, 1, tq), F32),
            pltpu.VMEM((A_WIDTH, tq), F32),
            pltpu.VMEM((2, A_HEADS, tk, tq), F32),
        ],
        compiler_params=_cparams(("parallel", "arbitrary")),
        name="dsa_attention",
    )(p3, p3, p3, sm3, sm3, gk, kmat, vmat_t)


CONF_HALO = 32
CONF_ROWS = 64
CONF_WIN = CONF_ROWS + CONF_HALO


CONF_BLK = 128


def _conf_kernel(xn_ref, w_ref, cw_ref, cb_ref, lg_ref, lb_ref, o_ref, y_ref, hp_ref, c_ref, z_ref,
                 *, seq):
    tm = o_ref.shape[0]
    nb = tm // CONF_BLK
    first = CONF_HALO - (CONV_K - 1)

    def project(r, slot):
        r0 = pl.multiple_of(r * CONF_BLK, CONF_BLK)
        y_ref[slot] = _dot(xn_ref[pl.ds(r0, CONF_BLK), :], w_ref[...])

    def epilogue(r, slot):
        r0 = pl.multiple_of(r * CONF_BLK, CONF_BLK)
        glu = y_ref[slot, :, 0:B_WIDTH] * _sigmoid(y_ref[slot, :, B_WIDTH:2 * B_WIDTH])
        z_ref[...] = _silu(y_ref[slot, :, 2 * B_WIDTH:3 * B_WIDTH])
        history = hp_ref[CONF_BLK:CONF_BLK + CONF_HALO, :]
        hp_ref[0:CONF_HALO, :] = jnp.where(r % (seq // CONF_BLK) == 0, 0.0, history)
        hp_ref[CONF_HALO:CONF_HALO + CONF_BLK, :] = glu
        for ct in range(B_WIDTH // LANES):
            cs = slice(ct * LANES, (ct + 1) * LANES)
            for base in range(0, CONF_BLK, CONF_ROWS):
                win = hp_ref[base:base + CONF_WIN, cs]
                acc = jnp.zeros((CONF_ROWS, LANES), F32)
                for rem in range(SUBLANES):
                    rot = win if rem == 0 else pltpu.roll(win, CONF_WIN - rem, 0)
                    for j in range(CONV_K):
                        if (first + j) % SUBLANES == rem:
                            a = (first + j) - rem
                            acc = acc + cw_ref[j:j + 1, cs] * rot[a:a + CONF_ROWS]
                c_ref[base:base + CONF_ROWS, cs] = acc + cb_ref[:, cs]
        y = c_ref[...]
        mu = jnp.mean(y, axis=-1, keepdims=True)
        yc = y - mu
        var = jnp.mean(yc * yc, axis=-1, keepdims=True)
        yn = yc * lax.rsqrt(var + LN_EPS) * lg_ref[...] + lb_ref[...]
        o_ref[pl.ds(r0, CONF_BLK), :] = (_silu(yn) * z_ref[...]).astype(o_ref.dtype)

    def step(r, carry):
        slot = r % 2
        epilogue(r, slot)
        project(jnp.minimum(r + 1, nb - 1), 1 - slot)
        return carry

    hp_ref[...] = jnp.zeros(hp_ref.shape, F32)
    project(0, 0)
    lax.fori_loop(0, nb, step, 0)


def _conformer(xn, w, cw, cb, lg, lb, tm, seq):
    n, d = xn.shape
    kern = functools.partial(_conf_kernel, seq=seq)
    once = dict(pipeline_mode=pl.Buffered(1))
    vec = lambda r: pl.BlockSpec((r, B_WIDTH), lambda i: (0, 0), **once)
    return pl.pallas_call(
        kern,
        grid=(n // tm,),
        in_specs=[pl.BlockSpec((tm, d), lambda i: (i, 0)),
                  pl.BlockSpec((d, 3 * B_WIDTH), lambda i: (0, 0), **once),
                  vec(CONF_HALO), vec(1), vec(1), vec(1)],
        out_specs=pl.BlockSpec((tm, B_WIDTH), lambda i: (i, 0)),
        out_shape=jax.ShapeDtypeStruct((n, B_WIDTH), BF16),
        scratch_shapes=[pltpu.VMEM((2, CONF_BLK, 3 * B_WIDTH), F32),
                        pltpu.VMEM((CONF_HALO + CONF_BLK, B_WIDTH), F32),
                        pltpu.VMEM((CONF_BLK, B_WIDTH), F32),
                        pltpu.VMEM((CONF_BLK, B_WIDTH), F32)],
        compiler_params=_cparams(("parallel",)),
        name="conformer_proj",
    )(xn, w, cw, cb, lg, lb)


GDN_HALO = 8
QKV_ROWS = 128


def _qkv_kernel(xn_ref, w_ref, cw_ref, o_ref, *, seq):
    j = pl.program_id(1)
    tm, tn = o_ref.shape
    first = GDN_HALO - (SHORT_K - 1)
    tiles_per_part = C_WIDTH // tn
    is_v = j >= 2 * tiles_per_part
    qscale = jnp.where(j < tiles_per_part, C_HEAD_DIM ** -0.5, 1.0)
    rb = min(QKV_ROWS, seq)
    for s0 in range(0, tm, seq):
        tail = jnp.zeros((GDN_HALO, tn), F32)
        for r0 in range(s0, s0 + seq, rb):
            y = _dot(xn_ref[r0:r0 + rb, :], w_ref[...])
            xp = jnp.concatenate([tail, y], axis=0)
            tail = y[rb - GDN_HALO:, :]
            for ct in range(tn // LANES):
                cs = slice(ct * LANES, (ct + 1) * LANES)
                win = xp[:, cs]
                acc = cw_ref[SHORT_K - 1:SHORT_K, cs] * win[GDN_HALO:]
                for back in range(1, SHORT_K):
                    tap = SHORT_K - 1 - back
                    acc = acc + cw_ref[tap:tap + 1, cs] * pltpu.roll(win, back, 0)[GDN_HALO:]
                a = _silu(acc)
                rs = lax.rsqrt(jnp.sum(a * a, axis=-1, keepdims=True) + L2_EPS)
                o_ref[r0:r0 + rb, cs] = (a * (jnp.where(is_v, 1.0, rs) * qscale)).astype(BF16)


def _qkv_proj(xn, w, cw, tm, tn, seq):
    n, d = xn.shape
    cols = w.shape[1]
    kern = functools.partial(_qkv_kernel, seq=seq)
    return pl.pallas_call(
        kern,
        grid=(n // tm, cols // tn),
        in_specs=[
            pl.BlockSpec((tm, d), lambda i, j: (i, 0)),
            pl.BlockSpec((d, tn), lambda i, j: (0, j)),
            pl.BlockSpec((SUBLANES, tn), lambda i, j: (0, j)),
        ],
        out_specs=pl.BlockSpec((tm, tn), lambda i, j: (i, j)),
        out_shape=jax.ShapeDtypeStruct((n, cols), BF16),
        compiler_params=_cparams(("parallel", "arbitrary")),
        name="qkv_proj",
    )(xn, w, cw)


def _gdn_gate_kernel(sm_ref, a_ref, dt_ref, o_ref):
    t = sm_ref.shape[0]
    x = sm_ref[...]
    lane = lax.broadcasted_iota(jnp.int32, (1, SM_COLS), 1)
    beta = _sigmoid(x)
    y = x + dt_ref[...]
    softplus = jnp.maximum(y, 0.0) + jnp.log1p(jnp.exp(-jnp.abs(y)))
    g = -jnp.exp(a_ref[...]) * softplus
    row = lax.broadcasted_iota(jnp.int32, (CHUNK, CHUNK), 0)
    col = lax.broadcasted_iota(jnp.int32, (CHUNK, CHUNK), 1)
    tril = (row >= col).astype(F32)
    is_beta = (lane >= SM_BETA) & (lane < SM_BETA + C_HEADS)
    for c in range(t // CHUNK):
        rs = slice(c * CHUNK, (c + 1) * CHUNK)
        gc = jnp.dot(tril, g[rs], preferred_element_type=F32, precision=lax.Precision.HIGHEST)
        o_ref[rs, :] = jnp.where(is_beta, beta[rs], gc)


def _gdn_gate(sm3, a_vec, dt_vec):
    b, t, _ = sm3.shape
    return pl.pallas_call(
        _gdn_gate_kernel,
        grid=(b,),
        in_specs=[
            pl.BlockSpec((None, t, SM_COLS), lambda i: (i, 0, 0)),
            pl.BlockSpec((1, SM_COLS), lambda i: (0, 0)),
            pl.BlockSpec((1, SM_COLS), lambda i: (0, 0)),
        ],
        out_specs=pl.BlockSpec((None, t, SM_COLS), lambda i: (i, 0, 0)),
        out_shape=jax.ShapeDtypeStruct((b, t, SM_COLS), F32),
        compiler_params=_cparams(("parallel",)),
        name="gdn_gate",
    )(sm3, a_vec, dt_vec)


GDN_BATCH = 32


def _pair_mm(x, y, diag_mask):
    y_bd = jnp.where(diag_mask, jnp.concatenate([y, y], axis=1), 0.0).astype(BF16)
    return _bdot(x.astype(BF16), y_bd)


def _unit_lower_inverse_pairs(m, idx_r, idx_c, diag_mask):
    n = m.shape[1]
    mm = lambda x, y: _pair_mm(x, y, diag_mask)
    eye = (idx_r == idx_c).astype(F32)
    blk = lambda s: (idx_r // s) == (idx_c // s)
    m0 = jnp.where(blk(8), m, 0.0)
    a2 = mm(m0, m0)
    p = eye - m0
    p = p + mm(p, a2)
    p = p + mm(p, mm(a2, a2))
    s = 8
    while s < n:
        lo = jnp.where(blk(2 * s) & jnp.logical_not(blk(s)), m, 0.0)
        p = p - mm(p, mm(lo, p))
        s *= 2
    return p


def _gdn_kernel(q_ref, k_ref, v_ref, z_ref, bcol_ref, gcol_ref, grow_ref, on_ref, o_ref,
                a_ref, b_ref, qp_ref, op_ref, s_ref, eg_ref, st_ref, *, hb, hg):
    tb = q_ref.shape[0]
    nc = tb // CHUNK
    idx_r = lax.broadcasted_iota(jnp.int32, (1, CHUNK, 2 * CHUNK), 1)
    lane = lax.broadcasted_iota(jnp.int32, (1, CHUNK, 2 * CHUNK), 2)
    odd = lane >= CHUNK
    idx_c = lane % CHUNK
    causal = idx_r >= idx_c
    strict = idx_r > idx_c
    diag_mask = (lax.broadcasted_iota(jnp.int32, (1, 2 * CHUNK, 2 * CHUNK), 1) >= CHUNK) == (
        lax.broadcasted_iota(jnp.int32, (1, 2 * CHUNK, 2 * CHUNK), 2) >= CHUNK)

    @pl.when(pl.program_id(2) == 0)
    def _():
        st_ref[...] = jnp.zeros(st_ref.shape, F32)

    def split(a):
        a2 = a.reshape((a.shape[0] // 2, 2) + a.shape[1:])
        return a2[:, 0], a2[:, 1]

    def merge(even, odd_):
        return jnp.stack([even, odd_], axis=1).reshape((2 * even.shape[0],) + even.shape[1:])

    def block_rows(even, odd_):
        zero = jnp.zeros_like(even)
        return jnp.concatenate([jnp.concatenate([even, zero], axis=-1),
                                jnp.concatenate([zero, odd_], axis=-1)], axis=1)

    heads = lambda ref: lambda h: ref[:, h * C_HEAD_DIM:(h + 1) * C_HEAD_DIM].astype(F32).reshape(
        nc, CHUNK, C_HEAD_DIM)
    cols = lambda ref: lambda h: ref[:, h:h + 1].reshape(nc, CHUNK, 1)
    for h0 in range(0, hb, hg):
        stack = lambda f: jnp.concatenate([f(h) for h in range(h0, h0 + hg)], axis=0)
        hsl = slice(h0, h0 + hg)
        q = stack(heads(q_ref))
        k = stack(heads(k_ref))
        v = stack(heads(v_ref))
        beta = stack(cols(bcol_ref))
        gc = stack(cols(gcol_ref))
        gr = stack(lambda h: grow_ref[h])
        gc_e, gc_o = split(gc)
        decay = jnp.exp(jnp.where(causal, jnp.where(odd, gc_o, gc_e) - gr, -jnp.inf))
        kb = k * beta
        k_e, k_o = split(k.astype(BF16))
        kb_e, kb_o = split(kb.astype(BF16))
        q_e, q_o = split(q.astype(BF16))
        k_blk = block_rows(k_e, k_o)
        m = jnp.where(strict, _bdot_nt(jnp.concatenate([kb_e, kb_o], axis=-1), k_blk) * decay, 0.0)
        tinv = _unit_lower_inverse_pairs(m, idx_r, idx_c, diag_mask).astype(BF16)
        rhs_e, rhs_o = split(jnp.concatenate([v * beta, kb * jnp.exp(gc)], axis=-1).astype(BF16))
        sol_p = _bdot(tinv, block_rows(rhs_e, rhs_o)).astype(BF16)
        sol_e, sol_o = sol_p[:, :, :2 * C_HEAD_DIM], sol_p[:, :, 2 * C_HEAD_DIM:]
        aqk = (_bdot_nt(jnp.concatenate([q_e, q_o], axis=-1), k_blk) * decay).astype(BF16)
        aqk_p = _bdot(aqk, block_rows(sol_e, sol_o))
        aqk_uw = merge(aqk_p[:, :, :2 * C_HEAD_DIM], aqk_p[:, :, 2 * C_HEAD_DIM:])
        glast = gc[:, CHUNK - 1:CHUNK, :]
        kd = (k * jnp.exp(glast - gc)).astype(BF16)
        unstack = lambda a: a.reshape((hg, nc) + a.shape[1:])
        kd_uw = _bdot_tn(kd, merge(sol_e, sol_o))
        a_ref[hsl] = unstack((-kd_uw[:, :, C_HEAD_DIM:]).astype(BF16))
        b_ref[hsl] = unstack(kd_uw[:, :, :C_HEAD_DIM].astype(BF16))
        qp_ref[hsl] = unstack((q * jnp.exp(gc) - aqk_uw[:, :, C_HEAD_DIM:]).astype(BF16))
        op_ref[hsl] = unstack(aqk_uw[:, :, :C_HEAD_DIM])
        eg_ref[hsl] = unstack(jnp.broadcast_to(jnp.exp(glast), (hg * nc, 1, C_HEAD_DIM)))

    def step(c, carry):
        for h in range(hb):
            s = st_ref[h]
            sb = s.astype(BF16)
            s_ref[h, c] = sb
            st_ref[h] = s * eg_ref[h, c] + _dot(a_ref[h, c], sb) + b_ref[h, c].astype(F32)
        return carry

    lax.fori_loop(0, nc, step, 0)

    for h in range(hb):
        hs = slice(h * C_HEAD_DIM, (h + 1) * C_HEAD_DIM)
        o = (_bdot(qp_ref[h], s_ref[h]) + op_ref[h]).reshape(tb, C_HEAD_DIM)
        ms = jnp.mean(o * o, axis=-1, keepdims=True)
        on = o * lax.rsqrt(ms + NORM_EPS) * on_ref[...]
        z = z_ref[:, hs].astype(F32)
        o_ref[:, hs] = (on * _silu(z)).astype(o_ref.dtype)


def _gdn(qkv3, p3, bcol, gcol, grow, on_g, hb, tb):
    b, t, _ = qkv3.shape
    nc = tb // CHUNK
    wd = hb * C_HEAD_DIM
    hg = min(hb, max(1, GDN_BATCH // nc))
    kern = functools.partial(_gdn_kernel, hb=hb, hg=hg)
    part = lambda which: pl.BlockSpec(
        (None, tb, wd), lambda i, j, s: (i, s, which * (C_WIDTH // wd) + j))
    return pl.pallas_call(
        kern,
        grid=(b, C_HEADS // hb, t // tb),
        in_specs=[
            part(0), part(1), part(2),
            pl.BlockSpec((None, tb, wd), lambda i, j, s: (i, s, OFF_ZC // wd + j)),
            pl.BlockSpec((None, None, tb, hb), lambda i, j, s: (i, j, s, 0)),
            pl.BlockSpec((None, None, tb, hb), lambda i, j, s: (i, j, s, 0)),
            pl.BlockSpec((None, hb, nc // 2, 1, 2 * CHUNK), lambda i, j, s: (i, j, s, 0, 0)),
            pl.BlockSpec((1, C_HEAD_DIM), lambda i, j, s: (0, 0)),
        ],
        out_specs=pl.BlockSpec((None, tb, wd), lambda i, j, s: (i, s, j)),
        out_shape=jax.ShapeDtypeStruct((b, t, C_WIDTH), BF16),
        scratch_shapes=[
            pltpu.VMEM((hb, nc, C_HEAD_DIM, C_HEAD_DIM), BF16),
            pltpu.VMEM((hb, nc, C_HEAD_DIM, C_HEAD_DIM), BF16),
            pltpu.VMEM((hb, nc, CHUNK, C_HEAD_DIM), BF16),
            pltpu.VMEM((hb, nc, CHUNK, C_HEAD_DIM), F32),
            pltpu.VMEM((hb, nc, C_HEAD_DIM, C_HEAD_DIM), BF16),
            pltpu.VMEM((hb, nc, 1, C_HEAD_DIM), F32),
            pltpu.VMEM((hb, C_HEAD_DIM, C_HEAD_DIM), F32),
        ],
        compiler_params=_cparams(("parallel", "parallel", "arbitrary")),
        name="gdn_delta_rule",
    )(qkv3, qkv3, qkv3, p3, bcol, gcol, grow, on_g)


def _merge_kernel(ua_ref, ub_ref, uc_ref, g0_ref, g1_ref, g2_ref, wa_ref, wb_ref, wc_ref, o_ref):
    ya = _dot(ua_ref[...], wa_ref[...])
    yb = _dot(ub_ref[...], wb_ref[...])
    yc = _dot(uc_ref[...], wc_ref[...])
    o = (_sigmoid(g0_ref[...].astype(F32)) * ya + _sigmoid(g1_ref[...].astype(F32)) * yb
         + _sigmoid(g2_ref[...].astype(F32)) * yc)
    o_ref[...] = o.astype(o_ref.dtype)


def _merge(ua, ub, uc, p, wa, wb, wc, tm, tn):
    n = ua.shape[0]
    gate = lambda br: pl.BlockSpec((tm, tn), lambda i, j: (i, (OFF_GATE + br * D_MODEL) // tn + j))
    return pl.pallas_call(
        _merge_kernel,
        grid=(n // tm, D_MODEL // tn),
        in_specs=[
            pl.BlockSpec((tm, A_WIDTH), lambda i, j: (i, 0)),
            pl.BlockSpec((tm, B_WIDTH), lambda i, j: (i, 0)),
            pl.BlockSpec((tm, C_WIDTH), lambda i, j: (i, 0)),
            gate(0), gate(1), gate(2),
            pl.BlockSpec((A_WIDTH, tn), lambda i, j: (0, j)),
            pl.BlockSpec((B_WIDTH, tn), lambda i, j: (0, j)),
            pl.BlockSpec((C_WIDTH, tn), lambda i, j: (0, j)),
        ],
        out_specs=pl.BlockSpec((tm, tn), lambda i, j: (i, j)),
        out_shape=jax.ShapeDtypeStruct((n, D_MODEL), BF16),
        compiler_params=_cparams(("parallel", "arbitrary")),
        name="gate_merge",
    )(ua, ub, uc, p, p, p, wa, wb, wc)


def _out_kernel(x_ref, m_ref, w_ref, g_ref, *o_refs, final):
    y = x_ref[...] + _dot(m_ref[...], w_ref[...])
    if final:
        o_refs[0][...] = _rmsnorm_rows(y, g_ref[...])
    else:
        o_refs[0][...] = y
        o_refs[1][...] = _rmsnorm_rows(y, g_ref[...]).astype(BF16)


def _out_proj(x2d, merged, w, g, tm, final):
    n, d = x2d.shape
    kern = functools.partial(_out_kernel, final=final)
    row = pl.BlockSpec((tm, d), lambda i: (i, 0))
    return pl.pallas_call(
        kern,
        grid=(n // tm,),
        in_specs=[row, row, pl.BlockSpec((d, d), lambda i: (0, 0)), pl.BlockSpec((1, d), lambda i: (0, 0))],
        out_specs=row if final else [row, row],
        out_shape=(jax.ShapeDtypeStruct((n, d), F32) if final else
                   [jax.ShapeDtypeStruct((n, d), F32), jax.ShapeDtypeStruct((n, d), BF16)]),
        compiler_params=_cparams(("parallel",)),
        name="out_proj",
    )(x2d, merged, w, g)


_SPLITS = (A_WIDTH, KV_RANK, IDX_HEADS * IDX_DIM, IDX_DIM, IDX_HEADS, A_WIDTH, B_WIDTH, B_WIDTH,
           B_WIDTH, 3 * C_WIDTH, C_HEADS, C_HEADS, C_WIDTH, N_BRANCH * D_MODEL)


def _repack_w_in(w):
    offs = np.concatenate([[0], np.cumsum(_SPLITS)])
    seg = [w[:, offs[i]:offs[i + 1]] for i in range(len(_SPLITS))]
    (q_a, c_kv, q_idx, k_idx, idx_w, z_a, glu_v, glu_g, z_b, qkv, beta, alpha, z_c, gates) = seg
    big = jnp.concatenate([q_a, q_idx, z_a, z_c, gates, c_kv], axis=1)
    pad = jnp.zeros((w.shape[0], SM_COLS - (SM_ALPHA + C_HEADS)), w.dtype)
    small = jnp.concatenate([k_idx, idx_w, beta, alpha, pad], axis=1)
    glu = jnp.concatenate([glu_v, glu_g, z_b], axis=1)
    return big.astype(BF16), small.astype(BF16), qkv.astype(BF16), glu.astype(BF16)


def _pad_lanes(vec, off):
    out = jnp.zeros((1, SM_COLS), F32)
    return out.at[0, off:off + vec.shape[0]].set(vec.astype(F32))


def _pick(total, pref):
    for c in pref:
        if total % c == 0:
            return c
    return total


def _tiles(n, t):
    return dict(tm_proj=t, tq=_pick(t, (256, 128)), tk=_pick(t, (256, 128)),
                tm=_pick(n, (512, 256)), tm_merge=_pick(n, (1024, 512, 256)), hb=8,
                tb=_pick(t, (1024, 512, 256, 128)))


def kernel(x, norm_g, w_in, kv_norm_g, idx_k_norm_g, w_uk, w_uv, w_proj_a, conv_w_b, conv_b_b,
           ln_g_b, ln_b_b, w_proj_b, conv_w_c, a_log, dt_bias, onorm_g_c, w_proj_c, w_out, final_g):
    b, t, _ = x.shape
    return _forward(_tiles(b * t, t), x, norm_g, w_in, kv_norm_g, idx_k_norm_g, w_uk, w_uv, w_proj_a,
                    conv_w_b, conv_b_b, ln_g_b, ln_b_b, w_proj_b, conv_w_c, a_log, dt_bias,
                    onorm_g_c, w_proj_c, w_out, final_g)


def _forward(tiles, x, norm_g, w_in, kv_norm_g, idx_k_norm_g, w_uk, w_uv, w_proj_a, conv_w_b,
             conv_b_b, ln_g_b, ln_b_b, w_proj_b, conv_w_c, a_log, dt_bias, onorm_g_c, w_proj_c,
             w_out, final_g):
    b, t, d = x.shape
    n = b * t
    depth = norm_g.shape[0]
    topk = min(TOPK_MAX, t // 4)
    nc = t // CHUNK
    tm_proj, tq, tk, tm, tm_merge, hb, tb = (
        tiles[k] for k in ("tm_proj", "tq", "tk", "tm", "tm_merge", "hb", "tb"))

    x2d = x.reshape(n, d)
    xn = _rmsnorm(x2d, norm_g[0][None, :].astype(F32), tm)
    for l in range(depth):
        w_big, w_small, w_qkv, w_glu = _repack_w_in(w_in[l])
        assert tm_proj % t == 0, "a projection row tile must hold whole sequences"
        p, sm = _proj(xn, w_big, w_small, tm_proj, 512)
        p3 = p.reshape(b, t, P_COLS)
        sm3 = sm.reshape(b, t, SM_COLS)

        kmat, vmat_t = _kv_expand(p3, kv_norm_g[l][None, :].astype(F32),
                                  w_uk[l].reshape(KV_RANK, A_WIDTH).astype(BF16),
                                  w_uv[l].reshape(KV_RANK, A_WIDTH).T.astype(BF16), tk)
        u_a = _dsa(p3, sm3, kmat, vmat_t, _pad_lanes(idx_k_norm_g[l], SM_KIDX), tq, tk, topk)

        cw_b = jnp.concatenate(
            [conv_w_b[l].astype(F32), jnp.zeros((CONF_HALO - CONV_K, B_WIDTH), F32)], axis=0)
        u_b = _conformer(xn, w_glu, cw_b, conv_b_b[l][None, :].astype(F32),
                         ln_g_b[l][None, :].astype(F32), ln_b_b[l][None, :].astype(F32), tm_proj, t)

        cw_c = jnp.concatenate(
            [conv_w_c[l].astype(F32), jnp.zeros((SUBLANES - SHORT_K, 3 * C_WIDTH), F32)], axis=0)
        qkv3 =_qkv_proj(xn, w_qkv, cw_c, tm_proj, 512, t).reshape(b, t, 3 * C_WIDTH)
        gate = _gdn_gate(sm3, _pad_lanes(a_log[l], SM_ALPHA), _pad_lanes(dt_bias[l], SM_ALPHA))
        beta = jnp.transpose(gate[:, :, SM_BETA:SM_BETA + C_HEADS], (0, 2, 1))
        gcum = jnp.transpose(gate[:, :, SM_ALPHA:SM_ALPHA + C_HEADS], (0, 2, 1))
        col = lambda a: jnp.transpose(a.reshape(b, C_HEADS // hb, hb, t), (0, 1, 3, 2))
        u_c = _gdn(qkv3, p3, col(beta), col(gcum),
                   gcum.reshape(b, C_HEADS, nc // 2, 1, 2 * CHUNK), onorm_g_c[l][None, :].astype(F32),
                   hb, tb)

        merged = _merge(u_a.reshape(n, A_WIDTH), u_b.reshape(n, B_WIDTH), u_c.reshape(n, C_WIDTH), p,
                        w_proj_a[l].astype(BF16), w_proj_b[l].astype(BF16), w_proj_c[l].astype(BF16),
                        tm_merge, 512)
        if l == depth - 1:
            x2d = _out_proj(x2d, merged, w_out[l].astype(BF16), final_g[None, :].astype(F32), tm, True)
        else:
            x2d, xn = _out_proj(x2d, merged, w_out[l].astype(BF16),
                                norm_g[l + 1][None, :].astype(F32), tm, False)
    return x2d.reshape(b, t, d)
```

```python
import functools

import jax
import jax.numpy as jnp
import numpy as np
from jax import lax
from jax.experimental import pallas as pl
from jax.experimental.pallas import tpu as pltpu

F32 = jnp.float32
BF16 = jnp.bfloat16

D_MODEL = 2048
NORM_EPS = 1e-6
LN_EPS = 1e-5
L2_EPS = 1e-6
A_HEADS = 8
A_HEAD_DIM = 128
A_WIDTH = A_HEADS * A_HEAD_DIM
KV_RANK = 512
IDX_HEADS = 16
IDX_DIM = 64
TOPK_MAX = 256
B_WIDTH = 1024
CONV_K = 31
C_HEADS = 16
C_HEAD_DIM = 128
C_WIDTH = C_HEADS * C_HEAD_DIM
SHORT_K = 4
CHUNK = 64
N_BRANCH = 3

LANES = 128
SUBLANES = 8
VMEM_LIMIT = 56 * 1024 * 1024

OFF_QA = 0
OFF_QIDX = OFF_QA + A_WIDTH
OFF_ZA = OFF_QIDX + IDX_HEADS * IDX_DIM
OFF_ZC = OFF_ZA + A_WIDTH
OFF_GATE = OFF_ZC + C_WIDTH
OFF_CKV = OFF_GATE + N_BRANCH * D_MODEL
P_COLS = OFF_CKV + KV_RANK
SM_KIDX = 0
SM_IDXW = SM_KIDX + IDX_DIM
SM_BETA = SM_IDXW + IDX_HEADS
SM_ALPHA = SM_BETA + C_HEADS
SM_COLS = LANES

INT_MIN = -(2**31)


def _cparams(sem):
    return pltpu.CompilerParams(dimension_semantics=sem, vmem_limit_bytes=VMEM_LIMIT)


def _sigmoid(x):
    return 1.0 / (1.0 + jnp.exp(-x))


def _silu(x):
    return x * _sigmoid(x)


def _dot(a, b):
    return jnp.dot(a, b, preferred_element_type=F32)


def _dot_nt(a, b):
    return lax.dot_general(a, b, (((1,), (1,)), ((), ())), preferred_element_type=F32)


def _bdot(a, b):
    return lax.dot_general(a, b, (((2,), (1,)), ((0,), (0,))), preferred_element_type=F32)


def _bdot_tn(a, b):
    return lax.dot_general(a, b, (((1,), (1,)), ((0,), (0,))), preferred_element_type=F32)


def _bdot_nt(a, b):
    return lax.dot_general(a, b, (((2,), (2,)), ((0,), (0,))), preferred_element_type=F32)


def _rmsnorm_rows(x, g):
    ms = jnp.mean(x * x, axis=-1, keepdims=True)
    return x * lax.rsqrt(ms + NORM_EPS) * g


def _rms_kernel(x_ref, g_ref, o_ref):
    o_ref[...] = _rmsnorm_rows(x_ref[...], g_ref[...]).astype(o_ref.dtype)


def _rmsnorm(x2d, g, tm):
    n, d = x2d.shape
    return pl.pallas_call(
        _rms_kernel,
        grid=(n // tm,),
        in_specs=[pl.BlockSpec((tm, d), lambda i: (i, 0)), pl.BlockSpec((1, d), lambda i: (0, 0))],
        out_specs=pl.BlockSpec((tm, d), lambda i: (i, 0)),
        out_shape=jax.ShapeDtypeStruct((n, d), BF16),
        compiler_params=_cparams(("parallel",)),
        name="rmsnorm",
    )(x2d, g)


def _proj_kernel(xn_ref, w_ref, ws_ref, o_ref, os_ref):
    xn = xn_ref[...]
    o_ref[...] = _dot(xn, w_ref[...]).astype(o_ref.dtype)

    @pl.when(pl.program_id(1) == 0)
    def _():
        os_ref[...] = _dot(xn, ws_ref[...])


def _proj(xn, w, w_small, tm, tn):
    n, d = xn.shape
    cols = w.shape[1]
    return pl.pallas_call(
        _proj_kernel,
        grid=(n // tm, cols // tn),
        in_specs=[
            pl.BlockSpec((tm, d), lambda i, j: (i, 0)),
            pl.BlockSpec((d, tn), lambda i, j: (0, j)),
            pl.BlockSpec((d, SM_COLS), lambda i, j: (0, 0)),
        ],
        out_specs=[pl.BlockSpec((tm, tn), lambda i, j: (i, j)),
                   pl.BlockSpec((tm, SM_COLS), lambda i, j: (i, 0))],
        out_shape=[jax.ShapeDtypeStruct((n, cols), BF16), jax.ShapeDtypeStruct((n, SM_COLS), F32)],
        compiler_params=_cparams(("parallel", "arbitrary")),
        name="in_proj",
    )(xn, w, w_small)


POS_SPLIT = 64
POS_TERMS = 3


def _kv_kernel(c_ref, g_ref, wk_ref, wvt_ref, k_ref, vt_ref):
    rows = c_ref.shape[0]
    chunks, _, tk = vt_ref.shape
    c = c_ref[...].astype(F32)
    ms = jnp.mean(c * c, axis=-1, keepdims=True)
    cn = (c * lax.rsqrt(ms + NORM_EPS) * g_ref[...]).astype(BF16)
    k = _dot(cn, wk_ref[...]).astype(BF16)
    for ch in range(chunks):
        vt_ref[ch] = _dot_nt(wvt_ref[...], cn[ch * tk:(ch + 1) * tk]).astype(BF16)
    pos = pl.program_id(1) * rows + lax.broadcasted_iota(jnp.int32, (rows, A_HEAD_DIM), 0)
    lane = lax.broadcasted_iota(jnp.int32, (rows, A_HEAD_DIM), 1)
    feat = jnp.where(lane < POS_TERMS, pos // POS_SPLIT,
                     jnp.where(lane < 2 * POS_TERMS, pos % POS_SPLIT, 0)).astype(F32).astype(BF16)
    for h in range(A_HEADS):
        k_ref[:, 2 * h * A_HEAD_DIM:(2 * h + 1) * A_HEAD_DIM] = k[:, h * A_HEAD_DIM:(h + 1) * A_HEAD_DIM]
        k_ref[:, (2 * h + 1) * A_HEAD_DIM:(2 * h + 2) * A_HEAD_DIM] = feat


def _kv_expand(p3, g, wk, wvt, tk, rows):
    b, t, _ = p3.shape
    return pl.pallas_call(
        _kv_kernel,
        grid=(b, t // rows),
        in_specs=[
            pl.BlockSpec((None, rows, KV_RANK), lambda i, j: (i, j, OFF_CKV // KV_RANK)),
            pl.BlockSpec((1, KV_RANK), lambda i, j: (0, 0)),
            pl.BlockSpec((KV_RANK, A_WIDTH), lambda i, j: (0, 0)),
            pl.BlockSpec((A_WIDTH, KV_RANK), lambda i, j: (0, 0)),
        ],
        out_specs=[
            pl.BlockSpec((None, rows, 2 * A_WIDTH), lambda i, j: (i, j, 0)),
            pl.BlockSpec((None, rows // tk, A_WIDTH, tk), lambda i, j: (i, j, 0, 0)),
        ],
        out_shape=[jax.ShapeDtypeStruct((b, t, 2 * A_WIDTH), BF16),
                   jax.ShapeDtypeStruct((b, t // tk, A_WIDTH, tk), BF16)],
        compiler_params=_cparams(("parallel", "parallel")),
        name="kv_expand",
    )(p3, g, wk, wvt)


LOG2E = 1.4426950408889634
MASK_BIAS = -1e30
M_INIT = -1e29
CNT_ROWS = 32
SEL_UNROLL = 4


def _dsa_kernel(qa_ref, qi_ref, za_ref, smq_ref, smk_ref, gk_ref, k_ref, vt_ref, o_ref,
                kn_ref, sc_ref, qs_ref, m_ref, l_ref, acc_ref, s_ref, *, tq, tk, topk):
    qb = pl.program_id(1)

    @pl.when(qb == 0)
    def _():
        kx = smk_ref[...]
        lane = lax.broadcasted_iota(jnp.int32, (1, SM_COLS), 1)
        is_k = lane < IDX_DIM
        kx = jnp.where(is_k, kx, 0.0)
        ms = jnp.sum(kx * kx, axis=-1, keepdims=True) * (1.0 / IDX_DIM)
        kn = kx * lax.rsqrt(ms + NORM_EPS) * gk_ref[...]
        kn_ref[0] = kn.astype(BF16)
        kn_ref[1] = pltpu.roll(kn, IDX_DIM, 1).astype(BF16)

    nkc = (qb * tq + tq + tk - 1) // tk
    qpos = qb * tq + lax.broadcasted_iota(jnp.int32, (1, tq), 1)
    kpos0 = lax.broadcasted_iota(jnp.int32, (tk, 1), 0)
    w_t = smq_ref[...].T * (IDX_HEADS ** -0.5 * IDX_DIM ** -0.5)

    def score_chunk(c, carry):
        k0 = pl.multiple_of(c * tk, tk)
        acc = jnp.zeros((tk, tq), F32)
        for hp in range(IDX_HEADS // 2):
            qpair = qi_ref[:, hp * LANES:(hp + 1) * LANES]
            for half in range(2):
                row = SM_IDXW + 2 * hp + half
                kc = kn_ref[half, pl.ds(k0, tk), :]
                acc = acc + w_t[row:row + 1, :] * jnp.maximum(_dot_nt(kc, qpair), 0.0)
        bits = pltpu.bitcast(acc, jnp.int32)
        skey = jnp.where(bits < 0, bits ^ jnp.int32(0x7FFFFFFF), bits)
        sc_ref[c] = jnp.where(k0 + kpos0 <= qpos, skey, jnp.int32(INT_MIN))
        return carry

    lax.fori_loop(0, nkc, score_chunk, 0)

    def count_ge(cand):
        def body(c, acc):
            ind = jnp.where(sc_ref[c] >= cand, 1.0, 0.0)
            return acc + jnp.sum(ind.reshape(tk // CNT_ROWS, CNT_ROWS, tq), axis=0)

        acc = lax.fori_loop(0, nkc, body, jnp.zeros((CNT_ROWS, tq), F32))
        return jnp.sum(acc, axis=0, keepdims=True)

    fkeep = jnp.float32(topk)

    def sel_cond(st):
        i, _, done = st
        return (i < 32) & (jnp.min(done) < 0.5)

    def sel_step(st):
        i, prefix, done = st
        for u in range(SEL_UNROLL):
            cand = prefix ^ (jnp.int32(1) << (31 - u - i))
            cnt = count_ge(cand)
            prefix = jnp.where((done < 0.5) & (cnt >= fkeep), cand, prefix)
            done = jnp.where(cnt == fkeep, 1.0, done)
        return i + SEL_UNROLL, prefix, done

    done0 = jnp.where(qpos < topk, 1.0, 0.0)
    _, kth, done = lax.while_loop(
        sel_cond, sel_step, (jnp.int32(0), jnp.full((1, tq), INT_MIN, jnp.int32), done0))
    thr = jnp.maximum(kth, jnp.int32(INT_MIN + 1))

    no_cut = jnp.full((1, tq), np.iinfo(np.int32).max, jnp.int32)

    def tie_cut(_):
        excess = jnp.where(done > 0.5, 0.0, count_ge(thr) - fkeep)

        def count_tied_from(pos):
            def body(c, acc):
                hit = (sc_ref[c] == thr) & (c * tk + kpos0 >= pos)
                ind = jnp.where(hit, 1.0, 0.0)
                return acc + jnp.sum(ind.reshape(tk // CNT_ROWS, CNT_ROWS, tq), axis=0)

            acc = lax.fori_loop(0, nkc, body, jnp.zeros((CNT_ROWS, tq), F32))
            return jnp.sum(acc, axis=0, keepdims=True)

        def pos_step(i, cut):
            cand = cut | (jnp.int32(1) << (pos_bits - 1 - i))
            return jnp.where(count_tied_from(cand) >= excess, cand, cut)

        cut = lax.fori_loop(0, pos_bits, pos_step, jnp.zeros((1, tq), jnp.int32))
        return jnp.where(excess > 0.5, cut, no_cut)

    pos_bits = max(1, (smk_ref.shape[0] - 1).bit_length())
    cut = lax.cond(jnp.min(done) < 0.5, tie_cut, lambda _: no_cut, 0)

    lane = lax.broadcasted_iota(jnp.int32, (1, A_HEAD_DIM), 1)
    for h in range(A_HEADS):
        slope = 2.0 ** (-8.0 * (h + 1) / A_HEADS) * LOG2E
        row = jnp.zeros((1, A_HEAD_DIM), F32)
        rest = np.float32(slope)
        for i in range(POS_TERMS):
            piece = np.float32(rest.astype(BF16))
            rest = np.float32(rest - piece)
            row = jnp.where(lane == i, float(piece) * POS_SPLIT, row)
            row = jnp.where(lane == POS_TERMS + i, float(piece), row)
        hs = slice(h * A_HEAD_DIM, (h + 1) * A_HEAD_DIM)
        qs_ref[:, 2 * h * A_HEAD_DIM:(2 * h + 1) * A_HEAD_DIM] = (
            qa_ref[:, hs].astype(F32) * (A_HEAD_DIM ** -0.5 * LOG2E)).astype(BF16)
        qs_ref[:, (2 * h + 1) * A_HEAD_DIM:(2 * h + 2) * A_HEAD_DIM] = jnp.broadcast_to(
            row, (tq, A_HEAD_DIM)).astype(BF16)
    m_ref[...] = jnp.full(m_ref.shape, M_INIT, F32)
    l_ref[...] = jnp.zeros(l_ref.shape, F32)
    acc_ref[...] = jnp.zeros(acc_ref.shape, F32)

    def qk_scores(c, slot):
        k0 = pl.multiple_of(c * tk, tk)
        for h in range(A_HEADS):
            hs2 = slice(2 * h * A_HEAD_DIM, (2 * h + 2) * A_HEAD_DIM)
            s_ref[slot, h] = _dot_nt(k_ref[pl.ds(k0, tk), hs2], qs_ref[:, hs2])

    def att_chunk(c, carry):
        slot = c % 2
        keys = sc_ref[c]
        keep = (keys > thr) | ((keys == thr) & (c * tk + kpos0 < cut))
        bias = jnp.where(keep, 0.0, MASK_BIAS)
        for h in range(A_HEADS):
            hs = slice(h * A_HEAD_DIM, (h + 1) * A_HEAD_DIM)
            s = s_ref[slot, h] + bias
            m_old = m_ref[h]
            m_new = jnp.maximum(m_old, jnp.max(s, axis=0, keepdims=True))
            p = jnp.exp2(s - m_new)
            alpha = jnp.exp2(m_old - m_new)
            l_ref[h] = alpha * l_ref[h] + jnp.sum(p, axis=0, keepdims=True)
            acc_ref[hs, :] = alpha * acc_ref[hs, :] + _dot(vt_ref[c, hs, :], p.astype(BF16))
            m_ref[h] = m_new
        qk_scores(jnp.minimum(c + 1, nkc - 1), 1 - slot)
        return carry

    qk_scores(0, 0)
    lax.fori_loop(0, nkc, att_chunk, 0)
    for h in range(A_HEADS):
        hs = slice(h * A_HEAD_DIM, (h + 1) * A_HEAD_DIM)
        z = za_ref[:, hs].astype(F32)
        o_ref[:, hs] = ((acc_ref[hs, :] / l_ref[h]).T * _silu(z)).astype(o_ref.dtype)


def _dsa(p3, sm3, kmat, vmat_t, gk, tq, tk, topk):
    b, t, _ = p3.shape
    kern = functools.partial(_dsa_kernel, tq=tq, tk=tk, topk=topk)
    return pl.pallas_call(
        kern,
        grid=(b, t // tq),
        in_specs=[
            pl.BlockSpec((None, tq, A_WIDTH), lambda i, j: (i, j, OFF_QA // A_WIDTH)),
            pl.BlockSpec((None, tq, A_WIDTH), lambda i, j: (i, j, OFF_QIDX // A_WIDTH)),
            pl.BlockSpec((None, tq, A_WIDTH), lambda i, j: (i, j, OFF_ZA // A_WIDTH)),
            pl.BlockSpec((None, tq, SM_COLS), lambda i, j: (i, j, 0)),
            pl.BlockSpec((None, t, SM_COLS), lambda i, j: (i, 0, 0)),
            pl.BlockSpec((1, SM_COLS), lambda i, j: (0, 0)),
            pl.BlockSpec((None, t, 2 * A_WIDTH), lambda i, j: (i, 0, 0)),
            pl.BlockSpec((None, t // tk, A_WIDTH, tk), lambda i, j: (i, 0, 0, 0)),
        ],
        out_specs=pl.BlockSpec((None, tq, A_WIDTH), lambda i, j: (i, j, 0)),
        out_shape=jax.ShapeDtypeStruct((b, t, A_WIDTH), BF16),
        scratch_shapes=[
            pltpu.VMEM((2, t, LANES), BF16),
            pltpu.VMEM((t // tk, tk, tq), jnp.int32),
            pltpu.VMEM((tq, 2 * A_WIDTH), BF16),
            pltpu.VMEM((A_HEADS, 1, tq), F32),
            pltpu.VMEM((A_HEADS, 1, tq), F32),
            pltpu.VMEM((A_WIDTH, tq), F32),
            pltpu.VMEM((2, A_HEADS, tk, tq), F32),
        ],
        compiler_params=_cparams(("parallel", "arbitrary")),
        name="dsa_attention",
    )(p3, p3, p3, sm3, sm3, gk, kmat, vmat_t)


CONF_HALO = 32
CONF_ROWS = 64
CONF_WIN = CONF_ROWS + CONF_HALO


CONF_BLK = 128


def _conf_kernel(xn_ref, w_ref, cw_ref, cb_ref, lg_ref, lb_ref, o_ref, y_ref, hp_ref, c_ref, z_ref,
                 *, seq):
    tm = o_ref.shape[0]
    nb = tm // CONF_BLK
    first = CONF_HALO - (CONV_K - 1)

    def project(r, slot):
        r0 = pl.multiple_of(r * CONF_BLK, CONF_BLK)
        y_ref[slot] = _dot(xn_ref[pl.ds(r0, CONF_BLK), :], w_ref[...])

    def epilogue(r, slot):
        r0 = pl.multiple_of(r * CONF_BLK, CONF_BLK)
        glu = y_ref[slot, :, 0:B_WIDTH] * _sigmoid(y_ref[slot, :, B_WIDTH:2 * B_WIDTH])
        z_ref[...] = _silu(y_ref[slot, :, 2 * B_WIDTH:3 * B_WIDTH])
        history = hp_ref[CONF_BLK:CONF_BLK + CONF_HALO, :]
        hp_ref[0:CONF_HALO, :] = jnp.where(r % (seq // CONF_BLK) == 0, 0.0, history)
        hp_ref[CONF_HALO:CONF_HALO + CONF_BLK, :] = glu
        for ct in range(B_WIDTH // LANES):
            cs = slice(ct * LANES, (ct + 1) * LANES)
            for base in range(0, CONF_BLK, CONF_ROWS):
                win = hp_ref[base:base + CONF_WIN, cs]
                acc = jnp.zeros((CONF_ROWS, LANES), F32)
                for rem in range(SUBLANES):
                    rot = win if rem == 0 else pltpu.roll(win, CONF_WIN - rem, 0)
                    for j in range(CONV_K):
                        if (first + j) % SUBLANES == rem:
                            a = (first + j) - rem
                            acc = acc + cw_ref[j:j + 1, cs] * rot[a:a + CONF_ROWS]
                c_ref[base:base + CONF_ROWS, cs] = acc + cb_ref[:, cs]
        y = c_ref[...]
        mu = jnp.mean(y, axis=-1, keepdims=True)
        yc = y - mu
        var = jnp.mean(yc * yc, axis=-1, keepdims=True)
        yn = yc * lax.rsqrt(var + LN_EPS) * lg_ref[...] + lb_ref[...]
        o_ref[pl.ds(r0, CONF_BLK), :] = (_silu(yn) * z_ref[...]).astype(o_ref.dtype)

    def step(r, carry):
        slot = r % 2
        epilogue(r, slot)
        project(jnp.minimum(r + 1, nb - 1), 1 - slot)
        return carry

    hp_ref[...] = jnp.zeros(hp_ref.shape, F32)
    project(0, 0)
    lax.fori_loop(0, nb, step, 0)


def _conformer(xn, w, cw, cb, lg, lb, tm, seq):
    n, d = xn.shape
    kern = functools.partial(_conf_kernel, seq=seq)
    once = dict(pipeline_mode=pl.Buffered(1))
    vec = lambda r: pl.BlockSpec((r, B_WIDTH), lambda i: (0, 0), **once)
    return pl.pallas_call(
        kern,
        grid=(n // tm,),
        in_specs=[pl.BlockSpec((tm, d), lambda i: (i, 0)),
                  pl.BlockSpec((d, 3 * B_WIDTH), lambda i: (0, 0), **once),
                  vec(CONF_HALO), vec(1), vec(1), vec(1)],
        out_specs=pl.BlockSpec((tm, B_WIDTH), lambda i: (i, 0)),
        out_shape=jax.ShapeDtypeStruct((n, B_WIDTH), BF16),
        scratch_shapes=[pltpu.VMEM((2, CONF_BLK, 3 * B_WIDTH), F32),
                        pltpu.VMEM((CONF_HALO + CONF_BLK, B_WIDTH), F32),
                        pltpu.VMEM((CONF_BLK, B_WIDTH), F32),
                        pltpu.VMEM((CONF_BLK, B_WIDTH), F32)],
        compiler_params=_cparams(("parallel",)),
        name="conformer_proj",
    )(xn, w, cw, cb, lg, lb)


GDN_HALO = 8
QKV_ROWS = 128


def _qkv_kernel(xn_ref, w_ref, cw_ref, o_ref, *, seq):
    j = pl.program_id(1)
    tm, tn = o_ref.shape
    first = GDN_HALO - (SHORT_K - 1)
    tiles_per_part = C_WIDTH // tn
    is_v = j >= 2 * tiles_per_part
    qscale = jnp.where(j < tiles_per_part, C_HEAD_DIM ** -0.5, 1.0)
    rb = min(QKV_ROWS, seq)
    for s0 in range(0, tm, seq):
        tail = jnp.zeros((GDN_HALO, tn), F32)
        for r0 in range(s0, s0 + seq, rb):
            y = _dot(xn_ref[r0:r0 + rb, :], w_ref[...])
            xp = jnp.concatenate([tail, y], axis=0)
            tail = y[rb - GDN_HALO:, :]
            for ct in range(tn // LANES):
                cs = slice(ct * LANES, (ct + 1) * LANES)
                win = xp[:, cs]
                acc = cw_ref[SHORT_K - 1:SHORT_K, cs] * win[GDN_HALO:]
                for back in range(1, SHORT_K):
                    tap = SHORT_K - 1 - back
                    acc = acc + cw_ref[tap:tap + 1, cs] * pltpu.roll(win, back, 0)[GDN_HALO:]
                a = _silu(acc)
                rs = lax.rsqrt(jnp.sum(a * a, axis=-1, keepdims=True) + L2_EPS)
                o_ref[r0:r0 + rb, cs] = (a * (jnp.where(is_v, 1.0, rs) * qscale)).astype(BF16)


def _qkv_proj(xn, w, cw, tm, tn, seq):
    n, d = xn.shape
    cols = w.shape[1]
    kern = functools.partial(_qkv_kernel, seq=seq)
    return pl.pallas_call(
        kern,
        grid=(n // tm, cols // tn),
        in_specs=[
            pl.BlockSpec((tm, d), lambda i, j: (i, 0)),
            pl.BlockSpec((d, tn), lambda i, j: (0, j)),
            pl.BlockSpec((SUBLANES, tn), lambda i, j: (0, j)),
        ],
        out_specs=pl.BlockSpec((tm, tn), lambda i, j: (i, j)),
        out_shape=jax.ShapeDtypeStruct((n, cols), BF16),
        compiler_params=_cparams(("parallel", "arbitrary")),
        name="qkv_proj",
    )(xn, w, cw)


def _gdn_gate_kernel(sm_ref, a_ref, dt_ref, o_ref):
    t = sm_ref.shape[0]
    x = sm_ref[...]
    lane = lax.broadcasted_iota(jnp.int32, (1, SM_COLS), 1)
    beta = _sigmoid(x)
    y = x + dt_ref[...]
    softplus = jnp.maximum(y, 0.0) + jnp.log1p(jnp.exp(-jnp.abs(y)))
    g = -jnp.exp(a_ref[...]) * softplus
    row = lax.broadcasted_iota(jnp.int32, (CHUNK, CHUNK), 0)
    col = lax.broadcasted_iota(jnp.int32, (CHUNK, CHUNK), 1)
    tril = (row >= col).astype(F32)
    is_beta = (lane >= SM_BETA) & (lane < SM_BETA + C_HEADS)
    for c in range(t // CHUNK):
        rs = slice(c * CHUNK, (c + 1) * CHUNK)
        gc = jnp.dot(tril, g[rs], preferred_element_type=F32, precision=lax.Precision.HIGHEST)
        o_ref[rs, :] = jnp.where(is_beta, beta[rs], gc)


def _gdn_gate(sm3, a_vec, dt_vec):
    b, t, _ = sm3.shape
    return pl.pallas_call(
        _gdn_gate_kernel,
        grid=(b,),
        in_specs=[
            pl.BlockSpec((None, t, SM_COLS), lambda i: (i, 0, 0)),
            pl.BlockSpec((1, SM_COLS), lambda i: (0, 0)),
            pl.BlockSpec((1, SM_COLS), lambda i: (0, 0)),
        ],
        out_specs=pl.BlockSpec((None, t, SM_COLS), lambda i: (i, 0, 0)),
        out_shape=jax.ShapeDtypeStruct((b, t, SM_COLS), F32),
        compiler_params=_cparams(("parallel",)),
        name="gdn_gate",
    )(sm3, a_vec, dt_vec)


GDN_BATCH = 32


def _pair_mm(x, y, diag_mask):
    y_bd = jnp.where(diag_mask, jnp.concatenate([y, y], axis=1), 0.0).astype(BF16)
    return _bdot(x.astype(BF16), y_bd)


def _unit_lower_inverse_pairs(m, idx_r, idx_c, diag_mask):
    n = m.shape[1]
    mm = lambda x, y: _pair_mm(x, y, diag_mask)
    eye = (idx_r == idx_c).astype(F32)
    blk = lambda s: (idx_r // s) == (idx_c // s)
    m0 = jnp.where(blk(8), m, 0.0)
    a2 = mm(m0, m0)
    p = eye - m0
    p = p + mm(p, a2)
    p = p + mm(p, mm(a2, a2))
    s = 8
    while s < n:
        lo = jnp.where(blk(2 * s) & jnp.logical_not(blk(s)), m, 0.0)
        p = p - mm(p, mm(lo, p))
        s *= 2
    return p


def _gdn_kernel(q_ref, k_ref, v_ref, z_ref, bcol_ref, gcol_ref, grow_ref, on_ref, o_ref,
                a_ref, b_ref, qp_ref, op_ref, s_ref, eg_ref, st_ref, *, hb, hg):
    tb = q_ref.shape[0]
    nc = tb // CHUNK
    idx_r = lax.broadcasted_iota(jnp.int32, (1, CHUNK, 2 * CHUNK), 1)
    lane = lax.broadcasted_iota(jnp.int32, (1, CHUNK, 2 * CHUNK), 2)
    odd = lane >= CHUNK
    idx_c = lane % CHUNK
    causal = idx_r >= idx_c
    strict = idx_r > idx_c
    diag_mask = (lax.broadcasted_iota(jnp.int32, (1, 2 * CHUNK, 2 * CHUNK), 1) >= CHUNK) == (
        lax.broadcasted_iota(jnp.int32, (1, 2 * CHUNK, 2 * CHUNK), 2) >= CHUNK)

    @pl.when(pl.program_id(2) == 0)
    def _():
        st_ref[...] = jnp.zeros(st_ref.shape, F32)

    def split(a):
        a2 = a.reshape((a.shape[0] // 2, 2) + a.shape[1:])
        return a2[:, 0], a2[:, 1]

    def merge(even, odd_):
        return jnp.stack([even, odd_], axis=1).reshape((2 * even.shape[0],) + even.shape[1:])

    def block_rows(even, odd_):
        zero = jnp.zeros_like(even)
        return jnp.concatenate([jnp.concatenate([even, zero], axis=-1),
                                jnp.concatenate([zero, odd_], axis=-1)], axis=1)

    heads = lambda ref: lambda h: ref[:, h * C_HEAD_DIM:(h + 1) * C_HEAD_DIM].astype(F32).reshape(
        nc, CHUNK, C_HEAD_DIM)
    cols = lambda ref: lambda h: ref[:, h:h + 1].reshape(nc, CHUNK, 1)
    for h0 in range(0, hb, hg):
        stack = lambda f: jnp.concatenate([f(h) for h in range(h0, h0 + hg)], axis=0)
        hsl = slice(h0, h0 + hg)
        q = stack(heads(q_ref))
        k = stack(heads(k_ref))
        v = stack(heads(v_ref))
        beta = stack(cols(bcol_ref))
        gc = stack(cols(gcol_ref))
        gr = stack(lambda h: grow_ref[h])
        gc_e, gc_o = split(gc)
        decay = jnp.exp(jnp.where(causal, jnp.where(odd, gc_o, gc_e) - gr, -jnp.inf))
        kb = k * beta
        k_e, k_o = split(k.astype(BF16))
        kb_e, kb_o = split(kb.astype(BF16))
        q_e, q_o = split(q.astype(BF16))
        k_blk = block_rows(k_e, k_o)
        m = jnp.where(strict, _bdot_nt(jnp.concatenate([kb_e, kb_o], axis=-1), k_blk) * decay, 0.0)
        tinv = _unit_lower_inverse_pairs(m, idx_r, idx_c, diag_mask).astype(BF16)
        rhs_e, rhs_o = split(jnp.concatenate([v * beta, kb * jnp.exp(gc)], axis=-1).astype(BF16))
        sol_p = _bdot(tinv, block_rows(rhs_e, rhs_o)).astype(BF16)
        sol_e, sol_o = sol_p[:, :, :2 * C_HEAD_DIM], sol_p[:, :, 2 * C_HEAD_DIM:]
        aqk = (_bdot_nt(jnp.concatenate([q_e, q_o], axis=-1), k_blk) * decay).astype(BF16)
        aqk_p = _bdot(aqk, block_rows(sol_e, sol_o))
        aqk_uw = merge(aqk_p[:, :, :2 * C_HEAD_DIM], aqk_p[:, :, 2 * C_HEAD_DIM:])
        glast = gc[:, CHUNK - 1:CHUNK, :]
        kd = (k * jnp.exp(glast - gc)).astype(BF16)
        unstack = lambda a: a.reshape((hg, nc) + a.shape[1:])
        kd_uw = _bdot_tn(kd, merge(sol_e, sol_o))
        a_ref[hsl] = unstack((-kd_uw[:, :, C_HEAD_DIM:]).astype(BF16))
        b_ref[hsl] = unstack(kd_uw[:, :, :C_HEAD_DIM].astype(BF16))
        qp_ref[hsl] = unstack((q * jnp.exp(gc) - aqk_uw[:, :, C_HEAD_DIM:]).astype(BF16))
        op_ref[hsl] = unstack(aqk_uw[:, :, :C_HEAD_DIM])
        eg_ref[hsl] = unstack(jnp.broadcast_to(jnp.exp(glast), (hg * nc, 1, C_HEAD_DIM)))

    def step(c, carry):
        for h in range(hb):
            s = st_ref[h]
            sb = s.astype(BF16)
            s_ref[h, c] = sb
            st_ref[h] = s * eg_ref[h, c] + _dot(a_ref[h, c], sb) + b_ref[h, c].astype(F32)
        return carry

    lax.fori_loop(0, nc, step, 0)

    for h in range(hb):
        hs = slice(h * C_HEAD_DIM, (h + 1) * C_HEAD_DIM)
        o = (_bdot(qp_ref[h], s_ref[h]) + op_ref[h]).reshape(tb, C_HEAD_DIM)
        ms = jnp.mean(o * o, axis=-1, keepdims=True)
        on = o * lax.rsqrt(ms + NORM_EPS) * on_ref[...]
        z = z_ref[:, hs].astype(F32)
        o_ref[:, hs] = (on * _silu(z)).astype(o_ref.dtype)


def _gdn(qkv3, p3, bcol, gcol, grow, on_g, hb, tb):
    b, t, _ = qkv3.shape
    nc = tb // CHUNK
    wd = hb * C_HEAD_DIM
    hg = min(hb, max(1, GDN_BATCH // nc))
    kern = functools.partial(_gdn_kernel, hb=hb, hg=hg)
    part = lambda which: pl.BlockSpec(
        (None, tb, wd), lambda i, j, s: (i, s, which * (C_WIDTH // wd) + j))
    return pl.pallas_call(
        kern,
        grid=(b, C_HEADS // hb, t // tb),
        in_specs=[
            part(0), part(1), part(2),
            pl.BlockSpec((None, tb, wd), lambda i, j, s: (i, s, OFF_ZC // wd + j)),
            pl.BlockSpec((None, None, tb, hb), lambda i, j, s: (i, j, s, 0)),
            pl.BlockSpec((None, None, tb, hb), lambda i, j, s: (i, j, s, 0)),
            pl.BlockSpec((None, hb, nc // 2, 1, 2 * CHUNK), lambda i, j, s: (i, j, s, 0, 0)),
            pl.BlockSpec((1, C_HEAD_DIM), lambda i, j, s: (0, 0)),
        ],
        out_specs=pl.BlockSpec((None, tb, wd), lambda i, j, s: (i, s, j)),
        out_shape=jax.ShapeDtypeStruct((b, t, C_WIDTH), BF16),
        scratch_shapes=[
            pltpu.VMEM((hb, nc, C_HEAD_DIM, C_HEAD_DIM), BF16),
            pltpu.VMEM((hb, nc, C_HEAD_DIM, C_HEAD_DIM), BF16),
            pltpu.VMEM((hb, nc, CHUNK, C_HEAD_DIM), BF16),
            pltpu.VMEM((hb, nc, CHUNK, C_HEAD_DIM), F32),
            pltpu.VMEM((hb, nc, C_HEAD_DIM, C_HEAD_DIM), BF16),
            pltpu.VMEM((hb, nc, 1, C_HEAD_DIM), F32),
            pltpu.VMEM((hb, C_HEAD_DIM, C_HEAD_DIM), F32),
        ],
        compiler_params=_cparams(("parallel", "parallel", "arbitrary")),
        name="gdn_delta_rule",
    )(qkv3, qkv3, qkv3, p3, bcol, gcol, grow, on_g)


def _merge_kernel(ua_ref, ub_ref, uc_ref, g0_ref, g1_ref, g2_ref, wa_ref, wb_ref, wc_ref, o_ref):
    ya = _dot(ua_ref[...], wa_ref[...])
    yb = _dot(ub_ref[...], wb_ref[...])
    yc = _dot(uc_ref[...], wc_ref[...])
    o = (_sigmoid(g0_ref[...].astype(F32)) * ya + _sigmoid(g1_ref[...].astype(F32)) * yb
         + _sigmoid(g2_ref[...].astype(F32)) * yc)
    o_ref[...] = o.astype(o_ref.dtype)


def _merge(ua, ub, uc, p, wa, wb, wc, tm, tn):
    n = ua.shape[0]
    gate = lambda br: pl.BlockSpec((tm, tn), lambda i, j: (i, (OFF_GATE + br * D_MODEL) // tn + j))
    return pl.pallas_call(
        _merge_kernel,
        grid=(n // tm, D_MODEL // tn),
        in_specs=[
            pl.BlockSpec((tm, A_WIDTH), lambda i, j: (i, 0)),
            pl.BlockSpec((tm, B_WIDTH), lambda i, j: (i, 0)),
            pl.BlockSpec((tm, C_WIDTH), lambda i, j: (i, 0)),
            gate(0), gate(1), gate(2),
            pl.BlockSpec((A_WIDTH, tn), lambda i, j: (0, j)),
            pl.BlockSpec((B_WIDTH, tn), lambda i, j: (0, j)),
            pl.BlockSpec((C_WIDTH, tn), lambda i, j: (0, j)),
        ],
        out_specs=pl.BlockSpec((tm, tn), lambda i, j: (i, j)),
        out_shape=jax.ShapeDtypeStruct((n, D_MODEL), BF16),
        compiler_params=_cparams(("parallel", "arbitrary")),
        name="gate_merge",
    )(ua, ub, uc, p, p, p, wa, wb, wc)


def _out_kernel(x_ref, m_ref, w_ref, g_ref, *o_refs, final):
    y = x_ref[...] + _dot(m_ref[...], w_ref[...])
    if final:
        o_refs[0][...] = _rmsnorm_rows(y, g_ref[...])
    else:
        o_refs[0][...] = y
        o_refs[1][...] = _rmsnorm_rows(y, g_ref[...]).astype(BF16)


def _out_proj(x2d, merged, w, g, tm, final):
    n, d = x2d.shape
    kern = functools.partial(_out_kernel, final=final)
    row = pl.BlockSpec((tm, d), lambda i: (i, 0))
    return pl.pallas_call(
        kern,
        grid=(n // tm,),
        in_specs=[row, row, pl.BlockSpec((d, d), lambda i: (0, 0)), pl.BlockSpec((1, d), lambda i: (0, 0))],
        out_specs=row if final else [row, row],
        out_shape=(jax.ShapeDtypeStruct((n, d), F32) if final else
                   [jax.ShapeDtypeStruct((n, d), F32), jax.ShapeDtypeStruct((n, d), BF16)]),
        compiler_params=_cparams(("parallel",)),
        name="out_proj",
    )(x2d, merged, w, g)


_SPLITS = (A_WIDTH, KV_RANK, IDX_HEADS * IDX_DIM, IDX_DIM, IDX_HEADS, A_WIDTH, B_WIDTH, B_WIDTH,
           B_WIDTH, 3 * C_WIDTH, C_HEADS, C_HEADS, C_WIDTH, N_BRANCH * D_MODEL)


def _repack_w_in(w):
    offs = np.concatenate([[0], np.cumsum(_SPLITS)])
    seg = [w[:, offs[i]:offs[i + 1]] for i in range(len(_SPLITS))]
    (q_a, c_kv, q_idx, k_idx, idx_w, z_a, glu_v, glu_g, z_b, qkv, beta, alpha, z_c, gates) = seg
    big = jnp.concatenate([q_a, q_idx, z_a, z_c, gates, c_kv], axis=1)
    pad = jnp.zeros((w.shape[0], SM_COLS - (SM_ALPHA + C_HEADS)), w.dtype)
    small = jnp.concatenate([k_idx, idx_w, beta, alpha, pad], axis=1)
    glu = jnp.concatenate([glu_v, glu_g, z_b], axis=1)
    return big.astype(BF16), small.astype(BF16), qkv.astype(BF16), glu.astype(BF16)


def _pad_lanes(vec, off):
    out = jnp.zeros((1, SM_COLS), F32)
    return out.at[0, off:off + vec.shape[0]].set(vec.astype(F32))


def _pick(total, pref):
    for c in pref:
        if total % c == 0:
            return c
    return total


def _tiles(n, t):
    return dict(tm_proj=t, tn=_pick(KV_RANK, (512, 256, 128)),
                tq=_pick(t, (256, 128)), tk=_pick(t, (256, 128)),
                tm=_pick(n, (512, 256)), tm_merge=_pick(n, (1024, 512, 256)), hb=8,
                tb=_pick(t, (1024, 512, 256, 128)))


def kernel(x, norm_g, w_in, kv_norm_g, idx_k_norm_g, w_uk, w_uv, w_proj_a, conv_w_b, conv_b_b,
           ln_g_b, ln_b_b, w_proj_b, conv_w_c, a_log, dt_bias, onorm_g_c, w_proj_c, w_out, final_g):
    b, t, _ = x.shape
    return _forward(_tiles(b * t, t), x, norm_g, w_in, kv_norm_g, idx_k_norm_g, w_uk, w_uv, w_proj_a,
                    conv_w_b, conv_b_b, ln_g_b, ln_b_b, w_proj_b, conv_w_c, a_log, dt_bias,
                    onorm_g_c, w_proj_c, w_out, final_g)


def _forward(tiles, x, norm_g, w_in, kv_norm_g, idx_k_norm_g, w_uk, w_uv, w_proj_a, conv_w_b,
             conv_b_b, ln_g_b, ln_b_b, w_proj_b, conv_w_c, a_log, dt_bias, onorm_g_c, w_proj_c,
             w_out, final_g):
    b, t, d = x.shape
    n = b * t
    depth = norm_g.shape[0]
    topk = min(TOPK_MAX, t // 4)
    nc = t // CHUNK
    tm_proj, tn, tq, tk, tm, tm_merge, hb, tb = (
        tiles[k] for k in ("tm_proj", "tn", "tq", "tk", "tm", "tm_merge", "hb", "tb"))

    x2d = x.reshape(n, d)
    xn = _rmsnorm(x2d, norm_g[0][None, :].astype(F32), tm)
    for l in range(depth):
        w_big, w_small, w_qkv, w_glu = _repack_w_in(w_in[l])
        assert tm_proj % t == 0, "a projection row tile must hold whole sequences"
        p, sm = _proj(xn, w_big, w_small, tm_proj, tn)
        p3 = p.reshape(b, t, P_COLS)
        sm3 = sm.reshape(b, t, SM_COLS)

        kmat, vmat_t = _kv_expand(p3, kv_norm_g[l][None, :].astype(F32),
                                  w_uk[l].reshape(KV_RANK, A_WIDTH).astype(BF16),
                                  w_uv[l].reshape(KV_RANK, A_WIDTH).T.astype(BF16), tk, tb)
        u_a = _dsa(p3, sm3, kmat, vmat_t, _pad_lanes(idx_k_norm_g[l], SM_KIDX), tq, tk, topk)

        cw_b = jnp.concatenate(
            [conv_w_b[l].astype(F32), jnp.zeros((CONF_HALO - CONV_K, B_WIDTH), F32)], axis=0)
        u_b = _conformer(xn, w_glu, cw_b, conv_b_b[l][None, :].astype(F32),
                         ln_g_b[l][None, :].astype(F32), ln_b_b[l][None, :].astype(F32), tm_proj, t)

        cw_c = jnp.concatenate(
            [conv_w_c[l].astype(F32), jnp.zeros((SUBLANES - SHORT_K, 3 * C_WIDTH), F32)], axis=0)
        qkv3 = _qkv_proj(xn, w_qkv, cw_c, tm_proj, tn, t).reshape(b, t, 3 * C_WIDTH)
        gate = _gdn_gate(sm3, _pad_lanes(a_log[l], SM_ALPHA), _pad_lanes(dt_bias[l], SM_ALPHA))
        beta = jnp.transpose(gate[:, :, SM_BETA:SM_BETA + C_HEADS], (0, 2, 1))
        gcum = jnp.transpose(gate[:, :, SM_ALPHA:SM_ALPHA + C_HEADS], (0, 2, 1))
        col = lambda a: jnp.transpose(a.reshape(b, C_HEADS // hb, hb, t), (0, 1, 3, 2))
        u_c = _gdn(qkv3, p3, col(beta), col(gcum),
                   gcum.reshape(b, C_HEADS, nc // 2, 1, 2 * CHUNK), onorm_g_c[l][None, :].astype(F32),
                   hb, tb)

        merged = _merge(u_a.reshape(n, A_WIDTH), u_b.reshape(n, B_WIDTH), u_c.reshape(n, C_WIDTH), p,
                        w_proj_a[l].astype(BF16), w_proj_b[l].astype(BF16), w_proj_c[l].astype(BF16),
                        tm_merge, tn)
        if l == depth - 1:
            x2d = _out_proj(x2d, merged, w_out[l].astype(BF16), final_g[None, :].astype(F32), tm, True)
        else:
            x2d, xn = _out_proj(x2d, merged, w_out[l].astype(BF16),
                                norm_g[l + 1][None, :].astype(F32), tm, False)
    return x2d.reshape(b, t, d)
```

```python
import functools

import jax
import jax.numpy as jnp
import numpy as np
from jax import lax
from jax.experimental import pallas as pl
from jax.experimental.pallas import tpu as pltpu

F32 = jnp.float32
BF16 = jnp.bfloat16

D_MODEL = 2048
NORM_EPS = 1e-6
LN_EPS = 1e-5
L2_EPS = 1e-6
A_HEADS = 8
A_HEAD_DIM = 128
A_WIDTH = A_HEADS * A_HEAD_DIM
KV_RANK = 512
IDX_HEADS = 16
IDX_DIM = 64
TOPK_MAX = 256
B_WIDTH = 1024
CONV_K = 31
C_HEADS = 16
C_HEAD_DIM = 128
C_WIDTH = C_HEADS * C_HEAD_DIM
SHORT_K = 4
CHUNK = 64
N_BRANCH = 3

LANES = 128
SUBLANES = 8
VMEM_LIMIT = 56 * 1024 * 1024

OFF_QA = 0
OFF_QIDX = OFF_QA + A_WIDTH
OFF_ZA = OFF_QIDX + IDX_HEADS * IDX_DIM
OFF_ZC = OFF_ZA + A_WIDTH
OFF_GATE = OFF_ZC + C_WIDTH
OFF_CKV = OFF_GATE + N_BRANCH * D_MODEL
P_COLS = OFF_CKV + KV_RANK
SM_KIDX = 0
SM_IDXW = SM_KIDX + IDX_DIM
SM_BETA = SM_IDXW + IDX_HEADS
SM_ALPHA = SM_BETA + C_HEADS
SM_COLS = LANES

INT_MIN = -(2**31)


def _cparams(sem):
    return pltpu.CompilerParams(dimension_semantics=sem, vmem_limit_bytes=VMEM_LIMIT)


def _sigmoid(x):
    return 1.0 / (1.0 + jnp.exp(-x))


def _silu(x):
    return x * _sigmoid(x)


def _dot(a, b):
    return jnp.dot(a, b, preferred_element_type=F32)


def _dot_nt(a, b):
    return lax.dot_general(a, b, (((1,), (1,)), ((), ())), preferred_element_type=F32)


def _bdot(a, b):
    return lax.dot_general(a, b, (((2,), (1,)), ((0,), (0,))), preferred_element_type=F32)


def _bdot_tn(a, b):
    return lax.dot_general(a, b, (((1,), (1,)), ((0,), (0,))), preferred_element_type=F32)


def _bdot_nt(a, b):
    return lax.dot_general(a, b, (((2,), (2,)), ((0,), (0,))), preferred_element_type=F32)


def _rmsnorm_rows(x, g):
    ms = jnp.mean(x * x, axis=-1, keepdims=True)
    return x * lax.rsqrt(ms + NORM_EPS) * g


def _rms_kernel(x_ref, g_ref, o_ref):
    o_ref[...] = _rmsnorm_rows(x_ref[...], g_ref[...]).astype(o_ref.dtype)


def _rmsnorm(x2d, g, tm):
    n, d = x2d.shape
    return pl.pallas_call(
        _rms_kernel,
        grid=(n // tm,),
        in_specs=[pl.BlockSpec((tm, d), lambda i: (i, 0)), pl.BlockSpec((1, d), lambda i: (0, 0))],
        out_specs=pl.BlockSpec((tm, d), lambda i: (i, 0)),
        out_shape=jax.ShapeDtypeStruct((n, d), BF16),
        compiler_params=_cparams(("parallel",)),
        name="rmsnorm",
    )(x2d, g)


def _proj_kernel(xn_ref, w_ref, ws_ref, o_ref, os_ref):
    xn = xn_ref[...]
    o_ref[...] = _dot(xn, w_ref[...]).astype(o_ref.dtype)

    @pl.when(pl.program_id(1) == 0)
    def _():
        os_ref[...] = _dot(xn, ws_ref[...])


def _proj(xn, w, w_small, tm, tn):
    n, d = xn.shape
    cols = w.shape[1]
    return pl.pallas_call(
        _proj_kernel,
        grid=(n // tm, cols // tn),
        in_specs=[
            pl.BlockSpec((tm, d), lambda i, j: (i, 0)),
            pl.BlockSpec((d, tn), lambda i, j: (0, j)),
            pl.BlockSpec((d, SM_COLS), lambda i, j: (0, 0)),
        ],
        out_specs=[pl.BlockSpec((tm, tn), lambda i, j: (i, j)),
                   pl.BlockSpec((tm, SM_COLS), lambda i, j: (i, 0))],
        out_shape=[jax.ShapeDtypeStruct((n, cols), BF16), jax.ShapeDtypeStruct((n, SM_COLS), F32)],
        compiler_params=_cparams(("parallel", "arbitrary")),
        name="in_proj",
    )(xn, w, w_small)


POS_SPLIT = 64
POS_TERMS = 3
V_SLAB = A_HEAD_DIM + 16
V_ROWS = A_HEADS * V_SLAB


def _kv_kernel(c_ref, g_ref, wk_ref, wvt_ref, k_ref, vt_ref):
    rows = c_ref.shape[0]
    chunks, _, tk = vt_ref.shape
    c = c_ref[...].astype(F32)
    ms = jnp.mean(c * c, axis=-1, keepdims=True)
    cn = (c * lax.rsqrt(ms + NORM_EPS) * g_ref[...]).astype(BF16)
    k = _dot(cn, wk_ref[...]).astype(BF16)
    ones = jnp.ones((V_SLAB - A_HEAD_DIM, tk), BF16)
    for ch in range(chunks):
        vt = _dot_nt(wvt_ref[...], cn[ch * tk:(ch + 1) * tk]).astype(BF16)
        for h in range(A_HEADS):
            vt_ref[ch, h * V_SLAB:h * V_SLAB + A_HEAD_DIM, :] = vt[h * A_HEAD_DIM:(h + 1) * A_HEAD_DIM]
            vt_ref[ch, h * V_SLAB + A_HEAD_DIM:(h + 1) * V_SLAB, :] = ones
    pos = pl.program_id(1) * rows + lax.broadcasted_iota(jnp.int32, (rows, A_HEAD_DIM), 0)
    lane = lax.broadcasted_iota(jnp.int32, (rows, A_HEAD_DIM), 1)
    feat = jnp.where(lane < POS_TERMS, pos // POS_SPLIT,
                     jnp.where(lane < 2 * POS_TERMS, pos % POS_SPLIT, 0)).astype(F32).astype(BF16)
    for h in range(A_HEADS):
        k_ref[:, 2 * h * A_HEAD_DIM:(2 * h + 1) * A_HEAD_DIM] = k[:, h * A_HEAD_DIM:(h + 1) * A_HEAD_DIM]
        k_ref[:, (2 * h + 1) * A_HEAD_DIM:(2 * h + 2) * A_HEAD_DIM] = feat


def _kv_expand(p3, g, wk, wvt, tk, rows):
    b, t, _ = p3.shape
    return pl.pallas_call(
        _kv_kernel,
        grid=(b, t // rows),
        in_specs=[
            pl.BlockSpec((None, rows, KV_RANK), lambda i, j: (i, j, OFF_CKV // KV_RANK)),
            pl.BlockSpec((1, KV_RANK), lambda i, j: (0, 0)),
            pl.BlockSpec((KV_RANK, A_WIDTH), lambda i, j: (0, 0)),
            pl.BlockSpec((A_WIDTH, KV_RANK), lambda i, j: (0, 0)),
        ],
        out_specs=[
            pl.BlockSpec((None, rows, 2 * A_WIDTH), lambda i, j: (i, j, 0)),
            pl.BlockSpec((None, rows // tk, V_ROWS, tk), lambda i, j: (i, j, 0, 0)),
        ],
        out_shape=[jax.ShapeDtypeStruct((b, t, 2 * A_WIDTH), BF16),
                   jax.ShapeDtypeStruct((b, t // tk, V_ROWS, tk), BF16)],
        compiler_params=_cparams(("parallel", "parallel")),
        name="kv_expand",
    )(p3, g, wk, wvt)


LOG2E = 1.4426950408889634
MASK_BIAS = -1e30
M_INIT = -1e29
CNT_ROWS = 32
SEL_UNROLL = 4
SEL_BLIND = 12


def _dsa_kernel(qa_ref, qi_ref, za_ref, smq_ref, smk_ref, gk_ref, k_ref, vt_ref, o_ref,
                kn_ref, sc_ref, qs_ref, m_ref, acc_ref, s_ref, *, tq, tk, topk):
    qb = pl.program_id(1)

    @pl.when(qb == 0)
    def _():
        kx = smk_ref[...]
        lane = lax.broadcasted_iota(jnp.int32, (1, SM_COLS), 1)
        is_k = lane < IDX_DIM
        kx = jnp.where(is_k, kx, 0.0)
        ms = jnp.sum(kx * kx, axis=-1, keepdims=True) * (1.0 / IDX_DIM)
        kn = kx * lax.rsqrt(ms + NORM_EPS) * gk_ref[...]
        kn_ref[0] = kn.astype(BF16)
        kn_ref[1] = pltpu.roll(kn, IDX_DIM, 1).astype(BF16)

    nkc = (qb * tq + tq + tk - 1) // tk
    qpos = qb * tq + lax.broadcasted_iota(jnp.int32, (1, tq), 1)
    kpos0 = lax.broadcasted_iota(jnp.int32, (tk, 1), 0)
    w_t = smq_ref[...].T * (IDX_HEADS ** -0.5 * IDX_DIM ** -0.5)

    def score_chunk(c, carry):
        k0 = pl.multiple_of(c * tk, tk)
        acc = jnp.zeros((tk, tq), F32)
        for hp in range(IDX_HEADS // 2):
            qpair = qi_ref[:, hp * LANES:(hp + 1) * LANES]
            for half in range(2):
                row = SM_IDXW + 2 * hp + half
                kc = kn_ref[half, pl.ds(k0, tk), :]
                acc = acc + w_t[row:row + 1, :] * jnp.maximum(_dot_nt(kc, qpair), 0.0)
        bits = pltpu.bitcast(acc, jnp.int32)
        skey = jnp.where(bits < 0, bits ^ jnp.int32(0x7FFFFFFF), bits)
        sc_ref[c] = jnp.where(k0 + kpos0 <= qpos, skey, jnp.int32(INT_MIN))
        return carry

    lax.fori_loop(0, nkc, score_chunk, 0)

    def count_ge(cand):
        def body(c, acc):
            ind = jnp.where(sc_ref[c] >= cand, 1.0, 0.0)
            return acc + jnp.sum(ind.reshape(tk // CNT_ROWS, CNT_ROWS, tq), axis=0)

        acc = lax.fori_loop(0, nkc, body, jnp.zeros((CNT_ROWS, tq), F32))
        return jnp.sum(acc, axis=0, keepdims=True)

    fkeep = jnp.float32(topk)

    def sel_cond(st):
        i, _, done = st
        return (i < 32) & (jnp.min(done) < 0.5)

    def sel_step(st):
        i, prefix, done = st
        for u in range(SEL_UNROLL):
            cand = prefix ^ (jnp.int32(1) << (31 - u - i))
            cnt = count_ge(cand)
            prefix = jnp.where((done < 0.5) & (cnt >= fkeep), cand, prefix)
            done = jnp.where(cnt == fkeep, 1.0, done)
        return i + SEL_UNROLL, prefix, done

    done0 = jnp.where(qpos < topk, 1.0, 0.0)
    state = (jnp.int32(0), jnp.full((1, tq), INT_MIN, jnp.int32), done0)
    state = lax.fori_loop(0, SEL_BLIND // SEL_UNROLL, lambda _, st: sel_step(st), state)
    _, kth, done = lax.while_loop(sel_cond, sel_step, state)
    thr = jnp.maximum(kth, jnp.int32(INT_MIN + 1))

    no_cut = jnp.full((1, tq), np.iinfo(np.int32).max, jnp.int32)

    def tie_cut(_):
        excess = jnp.where(done > 0.5, 0.0, count_ge(thr) - fkeep)

        def count_tied_from(pos):
            def body(c, acc):
                hit = (sc_ref[c] == thr) & (c * tk + kpos0 >= pos)
                ind = jnp.where(hit, 1.0, 0.0)
                return acc + jnp.sum(ind.reshape(tk // CNT_ROWS, CNT_ROWS, tq), axis=0)

            acc = lax.fori_loop(0, nkc, body, jnp.zeros((CNT_ROWS, tq), F32))
            return jnp.sum(acc, axis=0, keepdims=True)

        def pos_step(i, cut):
            cand = cut | (jnp.int32(1) << (pos_bits - 1 - i))
            return jnp.where(count_tied_from(cand) >= excess, cand, cut)

        cut = lax.fori_loop(0, pos_bits, pos_step, jnp.zeros((1, tq), jnp.int32))
        return jnp.where(excess > 0.5, cut, no_cut)

    pos_bits = max(1, (smk_ref.shape[0] - 1).bit_length())
    cut = lax.cond(jnp.min(done) < 0.5, tie_cut, lambda _: no_cut, 0)

    lane = lax.broadcasted_iota(jnp.int32, (1, A_HEAD_DIM), 1)
    for h in range(A_HEADS):
        slope = 2.0 ** (-8.0 * (h + 1) / A_HEADS) * LOG2E
        row = jnp.zeros((1, A_HEAD_DIM), F32)
        rest = np.float32(slope)
        for i in range(POS_TERMS):
            piece = np.float32(rest.astype(BF16))
            rest = np.float32(rest - piece)
            row = jnp.where(lane == i, float(piece) * POS_SPLIT, row)
            row = jnp.where(lane == POS_TERMS + i, float(piece), row)
        hs = slice(h * A_HEAD_DIM, (h + 1) * A_HEAD_DIM)
        qs_ref[:, 2 * h * A_HEAD_DIM:(2 * h + 1) * A_HEAD_DIM] = (
            qa_ref[:, hs].astype(F32) * (A_HEAD_DIM ** -0.5 * LOG2E)).astype(BF16)
        qs_ref[:, (2 * h + 1) * A_HEAD_DIM:(2 * h + 2) * A_HEAD_DIM] = jnp.broadcast_to(
            row, (tq, A_HEAD_DIM)).astype(BF16)
    m_ref[...] = jnp.full(m_ref.shape, M_INIT, F32)
    acc_ref[...] = jnp.zeros(acc_ref.shape, F32)

    def qk_scores(c, slot):
        k0 = pl.multiple_of(c * tk, tk)
        for h in range(A_HEADS):
            hs2 = slice(2 * h * A_HEAD_DIM, (2 * h + 2) * A_HEAD_DIM)
            s_ref[slot, h] = _dot_nt(k_ref[pl.ds(k0, tk), hs2], qs_ref[:, hs2])

    def att_chunk(c, carry):
        slot = c % 2
        keys = sc_ref[c]
        keep = (keys > thr) | ((keys == thr) & (c * tk + kpos0 < cut))
        bias = jnp.where(keep, 0.0, MASK_BIAS)
        for h in range(A_HEADS):
            hs = slice(h * A_HEAD_DIM, (h + 1) * A_HEAD_DIM)
            s = s_ref[slot, h] + bias
            m_old = m_ref[h]
            m_new = jnp.maximum(m_old, jnp.max(s, axis=0, keepdims=True))
            p = jnp.exp2(s - m_new)
            alpha = jnp.exp2(m_old - m_new)
            vs = slice(h * V_SLAB, (h + 1) * V_SLAB)
            acc_ref[vs, :] = alpha * acc_ref[vs, :] + _dot(vt_ref[c, vs, :], p.astype(BF16))
            m_ref[h] = m_new
        qk_scores(jnp.minimum(c + 1, nkc - 1), 1 - slot)
        return carry

    qk_scores(0, 0)
    lax.fori_loop(0, nkc, att_chunk, 0)
    for h in range(A_HEADS):
        hs = slice(h * A_HEAD_DIM, (h + 1) * A_HEAD_DIM)
        z = za_ref[:, hs].astype(F32)
        num = acc_ref[h * V_SLAB:h * V_SLAB + A_HEAD_DIM, :]
        den = acc_ref[h * V_SLAB + A_HEAD_DIM:h * V_SLAB + A_HEAD_DIM + 1, :]
        o_ref[:, hs] = ((num / den).T * _silu(z)).astype(o_ref.dtype)


def _dsa(p3, sm3, kmat, vmat_t, gk, tq, tk, topk):
    b, t, _ = p3.shape
    kern = functools.partial(_dsa_kernel, tq=tq, tk=tk, topk=topk)
    return pl.pallas_call(
        kern,
        grid=(b, t // tq),
        in_specs=[
            pl.BlockSpec((None, tq, A_WIDTH), lambda i, j: (i, j, OFF_QA // A_WIDTH)),
            pl.BlockSpec((None, tq, A_WIDTH), lambda i, j: (i, j, OFF_QIDX // A_WIDTH)),
            pl.BlockSpec((None, tq, A_WIDTH), lambda i, j: (i, j, OFF_ZA // A_WIDTH)),
            pl.BlockSpec((None, tq, SM_COLS), lambda i, j: (i, j, 0)),
            pl.BlockSpec((None, t, SM_COLS), lambda i, j: (i, 0, 0)),
            pl.BlockSpec((1, SM_COLS), lambda i, j: (0, 0)),
            pl.BlockSpec((None, t, 2 * A_WIDTH), lambda i, j: (i, 0, 0)),
            pl.BlockSpec((None, t // tk, V_ROWS, tk), lambda i, j: (i, 0, 0, 0)),
        ],
        out_specs=pl.BlockSpec((None, tq, A_WIDTH), lambda i, j: (i, j, 0)),
        out_shape=jax.ShapeDtypeStruct((b, t, A_WIDTH), BF16),
        scratch_shapes=[
            pltpu.VMEM((2, t, LANES), BF16),
            pltpu.VMEM((t // tk, tk, tq), jnp.int32),
            pltpu.VMEM((tq, 2 * A_WIDTH), BF16),
            pltpu.VMEM((A_HEADS, 1, tq), F32),
            pltpu.VMEM((V_ROWS, tq), F32),
            pltpu.VMEM((2, A_HEADS, tk, tq), F32),
        ],
        compiler_params=_cparams(("parallel", "arbitrary")),
        name="dsa_attention",
    )(p3, p3, p3, sm3, sm3, gk, kmat, vmat_t)


CONF_HALO = 32
CONF_ROWS = 64
CONF_WIN = CONF_ROWS + CONF_HALO


CONF_BLK = 128


def _conf_kernel(xn_ref, w_ref, cw_ref, cb_ref, lg_ref, lb_ref, o_ref, y_ref, hp_ref, c_ref, z_ref,
                 *, seq):
    tm = o_ref.shape[0]
    nb = tm // CONF_BLK
    first = CONF_HALO - (CONV_K - 1)

    def project(r, slot):
        r0 = pl.multiple_of(r * CONF_BLK, CONF_BLK)
        y_ref[slot] = _dot(xn_ref[pl.ds(r0, CONF_BLK), :], w_ref[...])

    def epilogue(r, slot):
        r0 = pl.multiple_of(r * CONF_BLK, CONF_BLK)
        glu = y_ref[slot, :, 0:B_WIDTH] * _sigmoid(y_ref[slot, :, B_WIDTH:2 * B_WIDTH])
        z_ref[...] = _silu(y_ref[slot, :, 2 * B_WIDTH:3 * B_WIDTH])
        history = hp_ref[CONF_BLK:CONF_BLK + CONF_HALO, :]
        hp_ref[0:CONF_HALO, :] = jnp.where(r % (seq // CONF_BLK) == 0, 0.0, history)
        hp_ref[CONF_HALO:CONF_HALO + CONF_BLK, :] = glu
        for ct in range(B_WIDTH // LANES):
            cs = slice(ct * LANES, (ct + 1) * LANES)
            for base in range(0, CONF_BLK, CONF_ROWS):
                win = hp_ref[base:base + CONF_WIN, cs]
                acc = jnp.zeros((CONF_ROWS, LANES), F32)
                for rem in range(SUBLANES):
                    rot = win if rem == 0 else pltpu.roll(win, CONF_WIN - rem, 0)
                    for j in range(CONV_K):
                        if (first + j) % SUBLANES == rem:
                            a = (first + j) - rem
                            acc = acc + cw_ref[j:j + 1, cs] * rot[a:a + CONF_ROWS]
                c_ref[base:base + CONF_ROWS, cs] = acc + cb_ref[:, cs]
        y = c_ref[...]
        mu = jnp.mean(y, axis=-1, keepdims=True)
        yc = y - mu
        var = jnp.mean(yc * yc, axis=-1, keepdims=True)
        yn = yc * lax.rsqrt(var + LN_EPS) * lg_ref[...] + lb_ref[...]
        o_ref[pl.ds(r0, CONF_BLK), :] = (_silu(yn) * z_ref[...]).astype(o_ref.dtype)

    def step(r, carry):
        slot = r % 2
        epilogue(r, slot)
        project(jnp.minimum(r + 1, nb - 1), 1 - slot)
        return carry

    hp_ref[...] = jnp.zeros(hp_ref.shape, F32)
    project(0, 0)
    lax.fori_loop(0, nb, step, 0)


def _conformer(xn, w, cw, cb, lg, lb, tm, seq):
    n, d = xn.shape
    kern = functools.partial(_conf_kernel, seq=seq)
    once = dict(pipeline_mode=pl.Buffered(1))
    vec = lambda r: pl.BlockSpec((r, B_WIDTH), lambda i: (0, 0), **once)
    return pl.pallas_call(
        kern,
        grid=(n // tm,),
        in_specs=[pl.BlockSpec((tm, d), lambda i: (i, 0)),
                  pl.BlockSpec((d, 3 * B_WIDTH), lambda i: (0, 0), **once),
                  vec(CONF_HALO), vec(1), vec(1), vec(1)],
        out_specs=pl.BlockSpec((tm, B_WIDTH), lambda i: (i, 0)),
        out_shape=jax.ShapeDtypeStruct((n, B_WIDTH), BF16),
        scratch_shapes=[pltpu.VMEM((2, CONF_BLK, 3 * B_WIDTH), F32),
                        pltpu.VMEM((CONF_HALO + CONF_BLK, B_WIDTH), F32),
                        pltpu.VMEM((CONF_BLK, B_WIDTH), F32),
                        pltpu.VMEM((CONF_BLK, B_WIDTH), F32)],
        compiler_params=_cparams(("parallel",)),
        name="conformer_proj",
    )(xn, w, cw, cb, lg, lb)


GDN_HALO = 8
QKV_ROWS = 128


def _qkv_kernel(xn_ref, w_ref, cw_ref, o_ref, *, seq):
    j = pl.program_id(1)
    tm, tn = o_ref.shape
    first = GDN_HALO - (SHORT_K - 1)
    tiles_per_part = C_WIDTH // tn
    is_v = j >= 2 * tiles_per_part
    qscale = jnp.where(j < tiles_per_part, C_HEAD_DIM ** -0.5, 1.0)
    rb = min(QKV_ROWS, seq)
    for s0 in range(0, tm, seq):
        tail = jnp.zeros((GDN_HALO, tn), F32)
        for r0 in range(s0, s0 + seq, rb):
            y = _dot(xn_ref[r0:r0 + rb, :], w_ref[...])
            xp = jnp.concatenate([tail, y], axis=0)
            tail = y[rb - GDN_HALO:, :]
            for ct in range(tn // LANES):
                cs = slice(ct * LANES, (ct + 1) * LANES)
                win = xp[:, cs]
                acc = cw_ref[SHORT_K - 1:SHORT_K, cs] * win[GDN_HALO:]
                for back in range(1, SHORT_K):
                    tap = SHORT_K - 1 - back
                    acc = acc + cw_ref[tap:tap + 1, cs] * pltpu.roll(win, back, 0)[GDN_HALO:]
                a = _silu(acc)
                rs = lax.rsqrt(jnp.sum(a * a, axis=-1, keepdims=True) + L2_EPS)
                o_ref[r0:r0 + rb, cs] = (a * (jnp.where(is_v, 1.0, rs) * qscale)).astype(BF16)


def _qkv_proj(xn, w, cw, tm, tn, seq):
    n, d = xn.shape
    cols = w.shape[1]
    kern = functools.partial(_qkv_kernel, seq=seq)
    return pl.pallas_call(
        kern,
        grid=(n // tm, cols // tn),
        in_specs=[
            pl.BlockSpec((tm, d), lambda i, j: (i, 0)),
            pl.BlockSpec((d, tn), lambda i, j: (0, j)),
            pl.BlockSpec((SUBLANES, tn), lambda i, j: (0, j)),
        ],
        out_specs=pl.BlockSpec((tm, tn), lambda i, j: (i, j)),
        out_shape=jax.ShapeDtypeStruct((n, cols), BF16),
        compiler_params=_cparams(("parallel", "arbitrary")),
        name="qkv_proj",
    )(xn, w, cw)


def _gdn_gate_kernel(sm_ref, a_ref, dt_ref, o_ref):
    t = sm_ref.shape[0]
    x = sm_ref[...]
    lane = lax.broadcasted_iota(jnp.int32, (1, SM_COLS), 1)
    beta = _sigmoid(x)
    y = x + dt_ref[...]
    softplus = jnp.maximum(y, 0.0) + jnp.log1p(jnp.exp(-jnp.abs(y)))
    g = -jnp.exp(a_ref[...]) * softplus
    row = lax.broadcasted_iota(jnp.int32, (CHUNK, CHUNK), 0)
    col = lax.broadcasted_iota(jnp.int32, (CHUNK, CHUNK), 1)
    tril = (row >= col).astype(F32)
    is_beta = (lane >= SM_BETA) & (lane < SM_BETA + C_HEADS)
    for c in range(t // CHUNK):
        rs = slice(c * CHUNK, (c + 1) * CHUNK)
        gc = jnp.dot(tril, g[rs], preferred_element_type=F32, precision=lax.Precision.HIGHEST)
        o_ref[rs, :] = jnp.where(is_beta, beta[rs], gc)


def _gdn_gate(sm3, a_vec, dt_vec):
    b, t, _ = sm3.shape
    return pl.pallas_call(
        _gdn_gate_kernel,
        grid=(b,),
        in_specs=[
            pl.BlockSpec((None, t, SM_COLS), lambda i: (i, 0, 0)),
            pl.BlockSpec((1, SM_COLS), lambda i: (0, 0)),
            pl.BlockSpec((1, SM_COLS), lambda i: (0, 0)),
        ],
        out_specs=pl.BlockSpec((None, t, SM_COLS), lambda i: (i, 0, 0)),
        out_shape=jax.ShapeDtypeStruct((b, t, SM_COLS), F32),
        compiler_params=_cparams(("parallel",)),
        name="gdn_gate",
    )(sm3, a_vec, dt_vec)


GDN_BATCH = 32


def _pair_mm(x, y, diag_mask):
    y_bd = jnp.where(diag_mask, jnp.concatenate([y, y], axis=1), 0.0).astype(BF16)
    return _bdot(x.astype(BF16), y_bd)


def _unit_lower_inverse_pairs(m, idx_r, idx_c, diag_mask):
    n = m.shape[1]
    mm = lambda x, y: _pair_mm(x, y, diag_mask)
    eye = (idx_r == idx_c).astype(F32)
    blk = lambda s: (idx_r // s) == (idx_c // s)
    m0 = jnp.where(blk(8), m, 0.0)
    a2 = mm(m0, m0)
    p = eye - m0
    p = p + mm(p, a2)
    p = p + mm(p, mm(a2, a2))
    s = 8
    while s < n:
        lo = jnp.where(blk(2 * s) & jnp.logical_not(blk(s)), m, 0.0)
        p = p - mm(p, mm(lo, p))
        s *= 2
    return p


def _gdn_kernel(q_ref, k_ref, v_ref, z_ref, bcol_ref, gcol_ref, grow_ref, on_ref, o_ref,
                a_ref, b_ref, qp_ref, op_ref, s_ref, eg_ref, st_ref, *, hb, hg):
    tb = q_ref.shape[0]
    nc = tb // CHUNK
    idx_r = lax.broadcasted_iota(jnp.int32, (1, CHUNK, 2 * CHUNK), 1)
    lane = lax.broadcasted_iota(jnp.int32, (1, CHUNK, 2 * CHUNK), 2)
    odd = lane >= CHUNK
    idx_c = lane % CHUNK
    causal = idx_r >= idx_c
    strict = idx_r > idx_c
    diag_mask = (lax.broadcasted_iota(jnp.int32, (1, 2 * CHUNK, 2 * CHUNK), 1) >= CHUNK) == (
        lax.broadcasted_iota(jnp.int32, (1, 2 * CHUNK, 2 * CHUNK), 2) >= CHUNK)

    @pl.when(pl.program_id(2) == 0)
    def _():
        st_ref[...] = jnp.zeros(st_ref.shape, F32)

    def split(a):
        a2 = a.reshape((a.shape[0] // 2, 2) + a.shape[1:])
        return a2[:, 0], a2[:, 1]

    def merge(even, odd_):
        return jnp.stack([even, odd_], axis=1).reshape((2 * even.shape[0],) + even.shape[1:])

    def block_rows(even, odd_):
        zero = jnp.zeros_like(even)
        return jnp.concatenate([jnp.concatenate([even, zero], axis=-1),
                                jnp.concatenate([zero, odd_], axis=-1)], axis=1)

    heads = lambda ref: lambda h: ref[:, h * C_HEAD_DIM:(h + 1) * C_HEAD_DIM].astype(F32).reshape(
        nc, CHUNK, C_HEAD_DIM)
    cols = lambda ref: lambda h: ref[:, h:h + 1].reshape(nc, CHUNK, 1)
    for h0 in range(0, hb, hg):
        stack = lambda f: jnp.concatenate([f(h) for h in range(h0, h0 + hg)], axis=0)
        hsl = slice(h0, h0 + hg)
        q = stack(heads(q_ref))
        k = stack(heads(k_ref))
        v = stack(heads(v_ref))
        beta = stack(cols(bcol_ref))
        gc = stack(cols(gcol_ref))
        gr = stack(lambda h: grow_ref[h])
        gc_e, gc_o = split(gc)
        decay = jnp.exp(jnp.where(causal, jnp.where(odd, gc_o, gc_e) - gr, -jnp.inf))
        kb = k * beta
        k_e, k_o = split(k.astype(BF16))
        kb_e, kb_o = split(kb.astype(BF16))
        q_e, q_o = split(q.astype(BF16))
        k_blk = block_rows(k_e, k_o)
        m = jnp.where(strict, _bdot_nt(jnp.concatenate([kb_e, kb_o], axis=-1), k_blk) * decay, 0.0)
        tinv = _unit_lower_inverse_pairs(m, idx_r, idx_c, diag_mask).astype(BF16)
        rhs_e, rhs_o = split(jnp.concatenate([v * beta, kb * jnp.exp(gc)], axis=-1).astype(BF16))
        sol_p = _bdot(tinv, block_rows(rhs_e, rhs_o)).astype(BF16)
        sol_e, sol_o = sol_p[:, :, :2 * C_HEAD_DIM], sol_p[:, :, 2 * C_HEAD_DIM:]
        aqk = (_bdot_nt(jnp.concatenate([q_e, q_o], axis=-1), k_blk) * decay).astype(BF16)
        aqk_p = _bdot(aqk, block_rows(sol_e, sol_o))
        aqk_uw = merge(aqk_p[:, :, :2 * C_HEAD_DIM], aqk_p[:, :, 2 * C_HEAD_DIM:])
        glast = gc[:, CHUNK - 1:CHUNK, :]
        kd = (k * jnp.exp(glast - gc)).astype(BF16)
        unstack = lambda a: a.reshape((hg, nc) + a.shape[1:])
        kd_uw = _bdot_tn(kd, merge(sol_e, sol_o))
        a_ref[hsl] = unstack((-kd_uw[:, :, C_HEAD_DIM:]).astype(BF16))
        b_ref[hsl] = unstack(kd_uw[:, :, :C_HEAD_DIM].astype(BF16))
        qp_ref[hsl] = unstack((q * jnp.exp(gc) - aqk_uw[:, :, C_HEAD_DIM:]).astype(BF16))
        op_ref[hsl] = unstack(aqk_uw[:, :, :C_HEAD_DIM])
        eg_ref[hsl] = unstack(jnp.broadcast_to(jnp.exp(glast), (hg * nc, 1, C_HEAD_DIM)))

    def step(c, carry):
        for h in range(hb):
            s = st_ref[h]
            sb = s.astype(BF16)
            s_ref[h, c] = sb
            st_ref[h] = s * eg_ref[h, c] + _dot(a_ref[h, c], sb) + b_ref[h, c].astype(F32)
        return carry

    lax.fori_loop(0, nc, step, 0)

    for h in range(hb):
        hs = slice(h * C_HEAD_DIM, (h + 1) * C_HEAD_DIM)
        o = (_bdot(qp_ref[h], s_ref[h]) + op_ref[h]).reshape(tb, C_HEAD_DIM)
        ms = jnp.mean(o * o, axis=-1, keepdims=True)
        on = o * lax.rsqrt(ms + NORM_EPS) * on_ref[...]
        z = z_ref[:, hs].astype(F32)
        o_ref[:, hs] = (on * _silu(z)).astype(o_ref.dtype)


def _gdn(qkv3, p3, bcol, gcol, grow, on_g, hb, tb):
    b, t, _ = qkv3.shape
    nc = tb // CHUNK
    wd = hb * C_HEAD_DIM
    hg = min(hb, max(1, GDN_BATCH // nc))
    kern = functools.partial(_gdn_kernel, hb=hb, hg=hg)
    part = lambda which: pl.BlockSpec(
        (None, tb, wd), lambda i, j, s: (i, s, which * (C_WIDTH // wd) + j))
    return pl.pallas_call(
        kern,
        grid=(b, C_HEADS // hb, t // tb),
        in_specs=[
            part(0), part(1), part(2),
            pl.BlockSpec((None, tb, wd), lambda i, j, s: (i, s, OFF_ZC // wd + j)),
            pl.BlockSpec((None, None, tb, hb), lambda i, j, s: (i, j, s, 0)),
            pl.BlockSpec((None, None, tb, hb), lambda i, j, s: (i, j, s, 0)),
            pl.BlockSpec((None, hb, nc // 2, 1, 2 * CHUNK), lambda i, j, s: (i, j, s, 0, 0)),
            pl.BlockSpec((1, C_HEAD_DIM), lambda i, j, s: (0, 0)),
        ],
        out_specs=pl.BlockSpec((None, tb, wd), lambda i, j, s: (i, s, j)),
        out_shape=jax.ShapeDtypeStruct((b, t, C_WIDTH), BF16),
        scratch_shapes=[
            pltpu.VMEM((hb, nc, C_HEAD_DIM, C_HEAD_DIM), BF16),
            pltpu.VMEM((hb, nc, C_HEAD_DIM, C_HEAD_DIM), BF16),
            pltpu.VMEM((hb, nc, CHUNK, C_HEAD_DIM), BF16),
            pltpu.VMEM((hb, nc, CHUNK, C_HEAD_DIM), F32),
            pltpu.VMEM((hb, nc, C_HEAD_DIM, C_HEAD_DIM), BF16),
            pltpu.VMEM((hb, nc, 1, C_HEAD_DIM), F32),
            pltpu.VMEM((hb, C_HEAD_DIM, C_HEAD_DIM), F32),
        ],
        compiler_params=_cparams(("parallel", "parallel", "arbitrary")),
        name="gdn_delta_rule",
    )(qkv3, qkv3, qkv3, p3, bcol, gcol, grow, on_g)


def _merge_kernel(ua_ref, ub_ref, uc_ref, g0_ref, g1_ref, g2_ref, wa_ref, wb_ref, wc_ref, o_ref):
    ya = _dot(ua_ref[...], wa_ref[...])
    yb = _dot(ub_ref[...], wb_ref[...])
    yc = _dot(uc_ref[...], wc_ref[...])
    o = (_sigmoid(g0_ref[...].astype(F32)) * ya + _sigmoid(g1_ref[...].astype(F32)) * yb
         + _sigmoid(g2_ref[...].astype(F32)) * yc)
    o_ref[...] = o.astype(o_ref.dtype)


def _merge(ua, ub, uc, p, wa, wb, wc, tm, tn):
    n = ua.shape[0]
    gate = lambda br: pl.BlockSpec((tm, tn), lambda i, j: (i, (OFF_GATE + br * D_MODEL) // tn + j))
    return pl.pallas_call(
        _merge_kernel,
        grid=(n // tm, D_MODEL // tn),
        in_specs=[
            pl.BlockSpec((tm, A_WIDTH), lambda i, j: (i, 0)),
            pl.BlockSpec((tm, B_WIDTH), lambda i, j: (i, 0)),
            pl.BlockSpec((tm, C_WIDTH), lambda i, j: (i, 0)),
            gate(0), gate(1), gate(2),
            pl.BlockSpec((A_WIDTH, tn), lambda i, j: (0, j)),
            pl.BlockSpec((B_WIDTH, tn), lambda i, j: (0, j)),
            pl.BlockSpec((C_WIDTH, tn), lambda i, j: (0, j)),
        ],
        out_specs=pl.BlockSpec((tm, tn), lambda i, j: (i, j)),
        out_shape=jax.ShapeDtypeStruct((n, D_MODEL), BF16),
        compiler_params=_cparams(("parallel", "arbitrary")),
        name="gate_merge",
    )(ua, ub, uc, p, p, p, wa, wb, wc)


def _out_kernel(x_ref, m_ref, w_ref, g_ref, *o_refs, final):
    y = x_ref[...] + _dot(m_ref[...], w_ref[...])
    if final:
        o_refs[0][...] = _rmsnorm_rows(y, g_ref[...])
    else:
        o_refs[0][...] = y
        o_refs[1][...] = _rmsnorm_rows(y, g_ref[...]).astype(BF16)


def _out_proj(x2d, merged, w, g, tm, final):
    n, d = x2d.shape
    kern = functools.partial(_out_kernel, final=final)
    row = pl.BlockSpec((tm, d), lambda i: (i, 0))
    return pl.pallas_call(
        kern,
        grid=(n // tm,),
        in_specs=[row, row, pl.BlockSpec((d, d), lambda i: (0, 0)), pl.BlockSpec((1, d), lambda i: (0, 0))],
        out_specs=row if final else [row, row],
        out_shape=(jax.ShapeDtypeStruct((n, d), F32) if final else
                   [jax.ShapeDtypeStruct((n, d), F32), jax.ShapeDtypeStruct((n, d), BF16)]),
        compiler_params=_cparams(("parallel",)),
        name="out_proj",
    )(x2d, merged, w, g)


_SPLITS = (A_WIDTH, KV_RANK, IDX_HEADS * IDX_DIM, IDX_DIM, IDX_HEADS, A_WIDTH, B_WIDTH, B_WIDTH,
           B_WIDTH, 3 * C_WIDTH, C_HEADS, C_HEADS, C_WIDTH, N_BRANCH * D_MODEL)


def _repack_w_in(w):
    offs = np.concatenate([[0], np.cumsum(_SPLITS)])
    seg = [w[:, offs[i]:offs[i + 1]] for i in range(len(_SPLITS))]
    (q_a, c_kv, q_idx, k_idx, idx_w, z_a, glu_v, glu_g, z_b, qkv, beta, alpha, z_c, gates) = seg
    big = jnp.concatenate([q_a, q_idx, z_a, z_c, gates, c_kv], axis=1)
    pad = jnp.zeros((w.shape[0], SM_COLS - (SM_ALPHA + C_HEADS)), w.dtype)
    small = jnp.concatenate([k_idx, idx_w, beta, alpha, pad], axis=1)
    glu = jnp.concatenate([glu_v, glu_g, z_b], axis=1)
    return big.astype(BF16), small.astype(BF16), qkv.astype(BF16), glu.astype(BF16)


def _pad_lanes(vec, off):
    out = jnp.zeros((1, SM_COLS), F32)
    return out.at[0, off:off + vec.shape[0]].set(vec.astype(F32))


def _pick(total, pref):
    for c in pref:
        if total % c == 0:
            return c
    return total


def _tiles(n, t):
    return dict(tm_proj=t, tn=_pick(KV_RANK, (512, 256, 128)),
                tq=_pick(t, (256, 128)), tk=_pick(t, (256, 128)),
                tm=_pick(n, (512, 256)), tm_merge=_pick(n, (1024, 512, 256)), hb=8,
                tb=_pick(t, (1024, 512, 256, 128)))


def kernel(x, norm_g, w_in, kv_norm_g, idx_k_norm_g, w_uk, w_uv, w_proj_a, conv_w_b, conv_b_b,
           ln_g_b, ln_b_b, w_proj_b, conv_w_c, a_log, dt_bias, onorm_g_c, w_proj_c, w_out, final_g):
    b, t, _ = x.shape
    return _forward(_tiles(b * t, t), x, norm_g, w_in, kv_norm_g, idx_k_norm_g, w_uk, w_uv, w_proj_a,
                    conv_w_b, conv_b_b, ln_g_b, ln_b_b, w_proj_b, conv_w_c, a_log, dt_bias,
                    onorm_g_c, w_proj_c, w_out, final_g)


def _forward(tiles, x, norm_g, w_in, kv_norm_g, idx_k_norm_g, w_uk, w_uv, w_proj_a, conv_w_b,
             conv_b_b, ln_g_b, ln_b_b, w_proj_b, conv_w_c, a_log, dt_bias, onorm_g_c, w_proj_c,
             w_out, final_g):
    b, t, d = x.shape
    n = b * t
    depth = norm_g.shape[0]
    topk = min(TOPK_MAX, t // 4)
    nc = t // CHUNK
    tm_proj, tn, tq, tk, tm, tm_merge, hb, tb = (
        tiles[k] for k in ("tm_proj", "tn", "tq", "tk", "tm", "tm_merge", "hb", "tb"))

    x2d = x.reshape(n, d)
    xn = _rmsnorm(x2d, norm_g[0][None, :].astype(F32), tm)
    for l in range(depth):
        w_big, w_small, w_qkv, w_glu = _repack_w_in(w_in[l])
        assert tm_proj % t == 0, "a projection row tile must hold whole sequences"
        p, sm = _proj(xn, w_big, w_small, tm_proj, tn)
        p3 = p.reshape(b, t, P_COLS)
        sm3 = sm.reshape(b, t, SM_COLS)

        kmat, vmat_t = _kv_expand(p3, kv_norm_g[l][None, :].astype(F32),
                                  w_uk[l].reshape(KV_RANK, A_WIDTH).astype(BF16),
                                  w_uv[l].reshape(KV_RANK, A_WIDTH).T.astype(BF16), tk, tb)
        u_a = _dsa(p3, sm3, kmat, vmat_t, _pad_lanes(idx_k_norm_g[l], SM_KIDX), tq, tk, topk)

        cw_b = jnp.concatenate(
            [conv_w_b[l].astype(F32), jnp.zeros((CONF_HALO - CONV_K, B_WIDTH), F32)], axis=0)
        u_b = _conformer(xn, w_glu, cw_b, conv_b_b[l][None, :].astype(F32),
                         ln_g_b[l][None, :].astype(F32), ln_b_b[l][None, :].astype(F32), tm_proj, t)

        cw_c = jnp.concatenate(
            [conv_w_c[l].astype(F32), jnp.zeros((SUBLANES - SHORT_K, 3 * C_WIDTH), F32)], axis=0)
        qkv3 = _qkv_proj(xn, w_qkv, cw_c, tm_proj, tn, t).reshape(b, t, 3 * C_WIDTH)
        gate = _gdn_gate(sm3, _pad_lanes(a_log[l], SM_ALPHA), _pad_lanes(dt_bias[l], SM_ALPHA))
        beta = jnp.transpose(gate[:, :, SM_BETA:SM_BETA + C_HEADS], (0, 2, 1))
        gcum = jnp.transpose(gate[:, :, SM_ALPHA:SM_ALPHA + C_HEADS], (0, 2, 1))
        col = lambda a: jnp.transpose(a.reshape(b, C_HEADS // hb, hb, t), (0, 1, 3, 2))
        u_c = _gdn(qkv3, p3, col(beta), col(gcum),
                   gcum.reshape(b, C_HEADS, nc // 2, 1, 2 * CHUNK), onorm_g_c[l][None, :].astype(F32),
                   hb, tb)

        merged = _merge(u_a.reshape(n, A_WIDTH), u_b.reshape(n, B_WIDTH), u_c.reshape(n, C_WIDTH), p,
                        w_proj_a[l].astype(BF16), w_proj_b[l].astype(BF16), w_proj_c[l].astype(BF16),
                        tm_merge, tn)
        if l == depth - 1:
            x2d = _out_proj(x2d, merged, w_out[l].astype(BF16), final_g[None, :].astype(F32), tm, True)
        else:
            x2d, xn = _out_proj(x2d, merged, w_out[l].astype(BF16),
                                norm_g[l + 1][None, :].astype(F32), tm, False)
    return x2d.reshape(b, t, d)
```
